```python
import jax, jax.numpy as jnp
from jax import lax
import numpy as np

D_MODEL = 1024
BATCH = 8
SEQ = 2048
DEPTH = 2

EPS = 1e-6
N_EVEN = (DEPTH + 1) // 2
N_ODD = DEPTH // 2

A_HEADS = 4
A_KDIM = 128
A_VDIM = 128
A_FDIM = A_HEADS * A_KDIM
A_WIDTH = A_HEADS * A_VDIM
CHUNK = 32
B_GROUPS = 4
B_GDIM = 128
B_WIDTH = B_GROUPS * B_GDIM
AB_SPLITS = (A_FDIM, 2 * A_FDIM, 3 * A_FDIM, 3 * A_FDIM + A_WIDTH, 3 * A_FDIM + 2 * A_WIDTH)
AB_IN = 3 * A_FDIM + 2 * A_WIDTH + B_WIDTH
AB_OUT = A_WIDTH + B_WIDTH
C_HEADS = 16
C_KV = 4
C_GROUP = C_HEADS // C_KV
C_HDIM = 64
C_QKV = (C_HEADS + 2 * C_KV) * C_HDIM
WINDOW = 128
QBLOCK = 128
D_FF = -(-8 * D_MODEL // (3 * 256)) * 256

kernel_name = "hybrid_hgrn2_fnet_swa_encoder"


def _rmsnorm(x, gain):
    xf = x.astype(jnp.float32)
    y = xf * lax.rsqrt(jnp.mean(xf * xf, axis=-1, keepdims=True) + EPS)
    return (y * gain.astype(jnp.float32)).astype(x.dtype)


def _gla_chunkwise(q, k, v, logf):
    b, h, t, dk = q.shape
    dv = v.shape[-1]
    n = t // CHUNK
    q, k, v, logf = (a.reshape(b, h, n, CHUNK, a.shape[-1]) for a in (q, k, v, logf))
    g = jnp.cumsum(logf, axis=3)
    g_ref = g[:, :, :, CHUNK // 2 - 1:CHUNK // 2, :]
    g_last = g[:, :, :, -1:, :]
    qr = q * jnp.exp(g - g_ref)
    kr = k * jnp.exp(g_ref - g)
    scores = jnp.einsum('bhnck,bhnsk->bhncs', qr, kr)
    causal_in_chunk = jnp.tril(jnp.ones((CHUNK, CHUNK), dtype=bool))
    scores = jnp.where(causal_in_chunk, scores, 0.0)
    o_intra = jnp.einsum('bhncs,bhnsv->bhncv', scores, v)
    q_in = q * jnp.exp(g)
    k_out = k * jnp.exp(g_last - g)
    decay = jnp.exp(g_last[:, :, :, 0, :])

    def step(state, xs):
        q_c, k_c, v_c, d_c = xs
        o_c = jnp.einsum('bhck,bhkv->bhcv', q_c, state)
        state = d_c[..., None] * state + jnp.einsum('bhck,bhcv->bhkv', k_c, v_c)
        return state, o_c

    xs = tuple(jnp.moveaxis(a, 2, 0) for a in (q_in, k_out, v, decay))
    s0 = jnp.zeros((b, h, dk, dv), jnp.float32)
    _, o_inter = lax.scan(step, s0, xs)
    o = o_intra + jnp.moveaxis(o_inter, 0, 2)
    return o.reshape(b, h, t, dv)


def _hgrn2_fourier_mixer(hn, w_in, lb, out_gain, w_out):
    b, t, _ = hn.shape
    proj = hn @ w_in
    q, zf, zb, iv, gate, u = jnp.split(proj, AB_SPLITS, axis=-1)

    def heads(a, d):
        return a.reshape(b, t, A_HEADS, d).transpose(0, 2, 1, 3).astype(jnp.float32)

    qh = heads(q, A_KDIM) * (A_KDIM ** -0.5)
    vh = heads(iv, A_VDIM)

    def gates(z, lb_dir):
        lb_h = lb_dir.reshape(A_HEADS, 1, A_KDIM)
        f = lb_h + (1.0 - lb_h) * jax.nn.sigmoid(heads(z, A_KDIM))
        return 1.0 - f, jnp.log(f)

    k_f, lf_f = gates(zf, lb[0])
    k_b, lf_b = gates(zb, lb[1])
    o_fwd = _gla_chunkwise(qh, k_f, vh, lf_f)
    flip = lambda a: jnp.flip(a, axis=2)
    o_bwd = flip(_gla_chunkwise(flip(qh), flip(k_b), flip(vh), flip(lf_b)))
    o = o_fwd + o_bwd
    o = o * lax.rsqrt(jnp.mean(o * o, axis=-1, keepdims=True) + EPS) * out_gain.astype(jnp.float32)[:, None, :]
    o = o.transpose(0, 2, 1, 3).reshape(b, t, A_WIDTH).astype(hn.dtype) * jax.nn.silu(gate)
    ug = u.reshape(b, t, B_GROUPS, B_GDIM).astype(jnp.float32)
    fo = jnp.fft.fft2(ug, axes=(1, 3), norm='ortho').real.reshape(b, t, B_WIDTH).astype(hn.dtype)
    return jnp.concatenate([o, fo], axis=-1) @ w_out


def _window_attention_mixer(hn, w_qkv, sink, w_out):
    b, t, _ = hn.shape
    qkv = hn @ w_qkv
    q, k, v = jnp.split(qkv, (C_HEADS * C_HDIM, (C_HEADS + C_KV) * C_HDIM), axis=-1)
    q = q.reshape(b, t, C_KV, C_GROUP, C_HDIM)
    k = k.reshape(b, t, C_KV, C_HDIM)
    v = v.reshape(b, t, C_KV, C_HDIM)
    pad = ((0, 0), (WINDOW, WINDOW), (0, 0), (0, 0))
    kp = jnp.pad(k, pad)
    vp = jnp.pad(v, pad)
    nb = t // QBLOCK
    span = QBLOCK + 2 * WINDOW
    qb = jnp.moveaxis(q.reshape(b, nb, QBLOCK, C_KV, C_GROUP, C_HDIM), 1, 0)
    slopes = (2.0 ** (-8.0 * jnp.arange(1, C_HEADS + 1, dtype=jnp.float32) / C_HEADS)).reshape(C_KV, C_GROUP, 1, 1)
    sink_logit = sink.astype(jnp.float32).reshape(C_KV, C_GROUP, 1, 1)
    scale = C_HDIM ** -0.5

    def block(args):
        j, q_blk = args
        start = j * QBLOCK
        k_blk = lax.dynamic_slice_in_dim(kp, start, span, axis=1)
        v_blk = lax.dynamic_slice_in_dim(vp, start, span, axis=1)
        qpos = start + jnp.arange(QBLOCK)
        kpos = start - WINDOW + jnp.arange(span)
        dist = jnp.abs(qpos[:, None] - kpos[None, :])
        valid = (dist <= WINDOW) & (kpos >= 0)[None, :] & (kpos < t)[None, :]
        s = jnp.einsum('bqkgd,bskd->bkgqs', q_blk, k_blk).astype(jnp.float32) * scale
        s = jnp.where(valid, s - slopes * dist.astype(jnp.float32), -jnp.inf)
        s = jnp.concatenate([s, jnp.broadcast_to(sink_logit, s.shape[:-1] + (1,))], axis=-1)
        p = jax.nn.softmax(s, axis=-1)[..., :-1]
        return jnp.einsum('bkgqs,bskd->bqkgd', p.astype(v_blk.dtype), v_blk)

    o = lax.map(block, (jnp.arange(nb), qb))
    o = jnp.moveaxis(o, 0, 1).reshape(b, t, C_HEADS * C_HDIM)
    return o @ w_out


def _swiglu(hn, w1, w3, w2):
    return (jax.nn.silu(hn @ w1) * (hn @ w3)) @ w2


def setup_inputs(seed: int = 0) -> dict:
    key = jax.random.key(seed)
    ks = jax.random.split(key, 13)
    f32 = jnp.float32
    nrm = lambda k, shape, fan_in: jax.random.normal(k, shape, f32) * (fan_in ** -0.5)
    return {
        "x": jax.random.normal(ks[0], (BATCH, SEQ, D_MODEL), f32),
        "norm_gains": 1.0 + 0.05 * jax.random.normal(ks[1], (DEPTH, 4, D_MODEL), f32),
        "ab_w_in": nrm(ks[2], (N_EVEN, D_MODEL, AB_IN), D_MODEL),
        "ab_lb_table": 0.5 * jax.random.normal(ks[3], (N_EVEN + 1, 2, A_FDIM), f32),
        "ab_out_gain": 1.0 + 0.05 * jax.random.normal(ks[4], (N_EVEN, A_HEADS, A_VDIM), f32),
        "ab_w_out": nrm(ks[5], (N_EVEN, AB_OUT, D_MODEL), AB_OUT),
        "c_w_qkv": nrm(ks[6], (N_ODD, D_MODEL, C_QKV), D_MODEL),
        "c_sink": 0.5 * jax.random.normal(ks[7], (N_ODD, C_HEADS), f32),
        "c_w_out": nrm(ks[8], (N_ODD, C_HEADS * C_HDIM, D_MODEL), C_HEADS * C_HDIM),
        "ffn_w1": nrm(ks[9], (DEPTH, D_MODEL, D_FF), D_MODEL),
        "ffn_w3": nrm(ks[10], (DEPTH, D_MODEL, D_FF), D_MODEL),
        "ffn_w2": nrm(ks[11], (DEPTH, D_FF, D_MODEL), D_FF),
    }


def reference(x, norm_gains, ab_w_in, ab_lb_table, ab_out_gain, ab_w_out, c_w_qkv, c_sink, c_w_out, ffn_w1, ffn_w3, ffn_w2):
    lb_all = jnp.cumsum(jax.nn.softmax(ab_lb_table.astype(jnp.float32), axis=0), axis=0)
    for layer in range(DEPTH):
        gains = norm_gains[layer]
        hn = _rmsnorm(x, gains[0])
        if layer % 2 == 0:
            e = layer // 2
            m = _hgrn2_fourier_mixer(hn, ab_w_in[e], lb_all[e], ab_out_gain[e], ab_w_out[e])
        else:
            o = layer // 2
            m = _window_attention_mixer(hn, c_w_qkv[o], c_sink[o], c_w_out[o])
        x = x + _rmsnorm(m, gains[1])
        hn = _rmsnorm(x, gains[2])
        x = x + _rmsnorm(_swiglu(hn, ffn_w1[layer], ffn_w3[layer], ffn_w2[layer]), gains[3])
    return x
```

```python
import functools

import numpy as np
import jax
import jax.numpy as jnp
from jax import lax
from jax.experimental import pallas as pl
from jax.experimental.pallas import tpu as pltpu

F32 = jnp.float32
BF16 = jnp.bfloat16

EPS = 1e-6
A_HEADS = 4
A_KDIM = 128
A_VDIM = 128
A_FDIM = A_HEADS * A_KDIM
A_WIDTH = A_HEADS * A_VDIM
SUB = 32
SUB_SHIFT = 5
NSUB = 4
GLA_CHUNK = SUB * NSUB
B_GROUPS = 4
B_GDIM = 128
B_WIDTH = B_GROUPS * B_GDIM
AB_IN = 3 * A_FDIM + 2 * A_WIDTH + B_WIDTH
C_HEADS = 16
C_KV = 4
C_GROUP = C_HEADS // C_KV
C_HDIM = 64
WINDOW = 128
QBLOCK = 128
NEG_BIG = -1e30

V7X_VMEM_LIMIT_BYTES = 56 * 1024 * 1024


def _params(*semantics):
    return pltpu.CompilerParams(dimension_semantics=semantics, vmem_limit_bytes=V7X_VMEM_LIMIT_BYTES)


def _rms_scale(y):
    return lax.rsqrt(jnp.mean(y * y, axis=-1, keepdims=True) + EPS)


def _norm_matmul_kernel(x_ref, g_ref, w_ref, o_ref, hn_ref):
    @pl.when(pl.program_id(1) == 0)
    def _():
        x = x_ref[...]
        hn_ref[...] = (x * _rms_scale(x) * g_ref[...]).astype(BF16)

    o_ref[...] = jnp.dot(hn_ref[...], w_ref[...].astype(BF16),
                         preferred_element_type=F32).astype(o_ref.dtype)


def _norm_matmul(x, gain, w, out_dtype, tm, tn):
    m, d = x.shape
    n = w.shape[1]
    return pl.pallas_call(
        _norm_matmul_kernel,
        grid=(m // tm, n // tn),
        in_specs=[
            pl.BlockSpec((tm, d), lambda i, j: (i, 0)),
            pl.BlockSpec((1, d), lambda i, j: (0, 0)),
            pl.BlockSpec((d, tn), lambda i, j: (0, j)),
        ],
        out_specs=pl.BlockSpec((tm, tn), lambda i, j: (i, j)),
        out_shape=jax.ShapeDtypeStruct((m, n), out_dtype),
        scratch_shapes=[pltpu.VMEM((tm, d), BF16)],
        compiler_params=_params("parallel", "arbitrary"),
        name="norm_matmul",
    )(x, gain.reshape(1, d), w)


def _proj_norm_res_kernel(*refs, n_in):
    a_refs, w_refs = refs[:n_in], refs[n_in:2 * n_in]
    x_ref, g_ref, o_ref = refs[2 * n_in:]
    y = None
    for a_ref, w_ref in zip(a_refs, w_refs):
        part = jnp.dot(a_ref[...], w_ref[...].astype(BF16), preferred_element_type=F32)
        y = part if y is None else y + part
    o_ref[...] = x_ref[...] + y * _rms_scale(y) * g_ref[...]


def _proj_norm_res(a_list, w, x, gain, tm):
    m, d = x.shape
    n_in = len(a_list)
    in_specs, args = [], []
    for a in a_list:
        in_specs.append(pl.BlockSpec((tm, a.shape[1]), lambda i: (i, 0)))
        args.append(a)
    k_piece = a_list[0].shape[1]
    assert all(a.shape[1] == k_piece for a in a_list) and k_piece * n_in == w.shape[0]
    for p in range(n_in):
        in_specs.append(pl.BlockSpec((k_piece, d), lambda i, p=p: (p, 0)))
        args.append(w)
    in_specs += [pl.BlockSpec((tm, d), lambda i: (i, 0)), pl.BlockSpec((1, d), lambda i: (0, 0))]
    args += [x, gain.reshape(1, d)]
    return pl.pallas_call(
        functools.partial(_proj_norm_res_kernel, n_in=n_in),
        grid=(m // tm,),
        in_specs=in_specs,
        out_specs=pl.BlockSpec((tm, d), lambda i: (i, 0)),
        out_shape=jax.ShapeDtypeStruct((m, d), F32),
        compiler_params=_params("parallel"),
        name="proj_norm_res",
    )(*args)


def _ffn_kernel(x_ref, g_in_ref, g_out_ref, w1_ref, w3_ref, w2_ref, o_ref, hn_ref, acc_ref):
    f = pl.program_id(1)

    @pl.when(f == 0)
    def _():
        x = x_ref[...]
        hn_ref[...] = (x * _rms_scale(x) * g_in_ref[...]).astype(BF16)

    hn = hn_ref[...]
    h1 = jnp.dot(hn, w1_ref[...].astype(BF16), preferred_element_type=F32)
    h3 = jnp.dot(hn, w3_ref[...].astype(BF16), preferred_element_type=F32)
    a = (h1 * jax.nn.sigmoid(h1) * h3).astype(BF16)
    part = jnp.dot(a, w2_ref[...].astype(BF16), preferred_element_type=F32)

    @pl.when(f == 0)
    def _():
        acc_ref[...] = part

    @pl.when(f > 0)
    def _():
        acc_ref[...] += part

    @pl.when(f == pl.num_programs(1) - 1)
    def _():
        y = acc_ref[...]
        o_ref[...] = x_ref[...] + y * _rms_scale(y) * g_out_ref[...]


def _ffn(x, g_in, g_out, w1, w3, w2, tm, tf):
    m, d = x.shape
    dff = w1.shape[1]
    return pl.pallas_call(
        _ffn_kernel,
        grid=(m // tm, dff // tf),
        in_specs=[
            pl.BlockSpec((tm, d), lambda i, f: (i, 0)),
            pl.BlockSpec((1, d), lambda i, f: (0, 0)),
            pl.BlockSpec((1, d), lambda i, f: (0, 0)),
            pl.BlockSpec((d, tf), lambda i, f: (0, f)),
            pl.BlockSpec((d, tf), lambda i, f: (0, f)),
            pl.BlockSpec((tf, d), lambda i, f: (f, 0)),
        ],
        out_specs=pl.BlockSpec((tm, d), lambda i, f: (i, 0)),
        out_shape=jax.ShapeDtypeStruct((m, d), F32),
        scratch_shapes=[pltpu.VMEM((tm, d), BF16), pltpu.VMEM((tm, d), F32)],
        compiler_params=_params("parallel", "arbitrary"),
        name="ffn",
    )(x, g_in.reshape(1, d), g_out.reshape(1, d), w1, w3, w2)


def _sub_blocks(a):
    return [a[i * SUB:(i + 1) * SUB] for i in range(NSUB)]


def _gla_chunk(q, z, v_t, lb, st, rev):
    L, K = q.shape
    f = lb + (1.0 - lb) * jax.nn.sigmoid(z)
    k = 1.0 - f
    lf = jnp.log(f)

    row = lax.broadcasted_iota(jnp.int32, (L, K), 0)
    ri = row & (SUB - 1)
    g = lf
    s = 1
    while s < SUB:
        if rev:
            g = g + jnp.where(ri < SUB - s, pltpu.roll(g, L - s, axis=0), 0.0)
        else:
            g = g + jnp.where(ri >= s, pltpu.roll(g, s, axis=0), 0.0)
        s *= 2

    order = list(range(NSUB))[::-1] if rev else list(range(NSUB))
    last, mid = (0, SUB // 2) if rev else (SUB - 1, SUB // 2 - 1)
    g_p = [_sub_blocks(g)[i] for i in order]
    q_p = [_sub_blocks(q)[i] for i in order]
    k_p = [_sub_blocks(k)[i] for i in order]
    tot = [gp[last:last + 1] for gp in g_p]
    ref = [gp[mid:mid + 1] for gp in g_p]

    qr = [q_p[p] * jnp.exp(g_p[p] - ref[p]) for p in range(NSUB)]
    kr = [k_p[p] * jnp.exp(ref[p] - g_p[p]) for p in range(NSUB)]
    qs = [q_p[p] * jnp.exp(g_p[p]) for p in range(NSUB)]
    ko = [k_p[p] * jnp.exp(tot[p] - g_p[p]) for p in range(NSUB)]
    e1, e2 = jnp.exp(tot[1]), jnp.exp(tot[2])
    e3 = jnp.exp(tot[3])
    qs2 = [qs[0], qs[1], qs[2], qs[3] * e2]
    ko2 = [ko[0] * e1, ko[1], ko[2], ko[3]]
    e23 = e2 * e3
    q_in = [qs[0], qs[1] * jnp.exp(tot[0]), qs[2] * jnp.exp(tot[0] + tot[1]),
            qs[3] * jnp.exp(tot[0] + tot[1] + tot[2])]
    k_out = [ko[0] * (e1 * e23), ko[1] * e23, ko[2] * e3, ko[3]]
    d_all = jnp.exp(tot[0] + tot[1] + tot[2] + tot[3])

    def rows(pieces):
        by_index = [None] * NSUB
        for p, i in enumerate(order):
            by_index[i] = pieces[p]
        return jnp.concatenate(by_index, axis=0).astype(BF16)

    nt = (((1,), (1,)), ((), ()))
    x_diag = lax.dot_general(rows(qr), rows(kr), nt, preferred_element_type=F32)
    x_adj = lax.dot_general(rows(qs), rows(ko), nt, preferred_element_type=F32)
    x_half = lax.dot_general(rows(qs2), rows(ko2), nt, preferred_element_type=F32)

    r = lax.broadcasted_iota(jnp.int32, (L, L), 0)
    c = lax.broadcasted_iota(jnp.int32, (L, L), 1)
    if rev:
        r, c = (L - 1) - r, (L - 1) - c
    bi, bj = r >> SUB_SHIFT, c >> SUB_SHIFT
    m_diag = (bi == bj) & (c <= r)
    m_adj = (bi == bj + 1) & ((bi & 1) == 1)
    m_half = (bi >= NSUB // 2) & (bj < NSUB // 2)
    a = jnp.where(m_diag, x_diag, jnp.where(m_adj, x_adj, jnp.where(m_half, x_half, 0.0)))

    lhs = jnp.concatenate([a.astype(BF16), rows(q_in)], axis=1)
    rhs_t = jnp.concatenate([v_t, st.astype(BF16)], axis=1)
    st_new = st * d_all + jnp.dot(v_t, rows(k_out), preferred_element_type=F32)
    return lhs, rhs_t, st_new


def _gla_kernel(q_ref, zf_ref, zb_ref, v_ref, gate_ref, lbt_ref, og_ref, o_ref, of_ref, ob_ref, *, layer_e):
    t_len = q_ref.shape[1]
    L = GLA_CHUNK
    nc = t_len // L
    nt = (((1,), (1,)), ((), ()))

    tab = lbt_ref[...]
    ex = jnp.exp(tab - jnp.max(tab, axis=0, keepdims=True))
    sm = ex / jnp.sum(ex, axis=0, keepdims=True)
    lb = sm[0]
    for i in range(1, layer_e + 1):
        lb = lb + sm[i]
    lb_f, lb_b = lb[0:1], lb[1:2]

    def chunk_rows(n):
        return pl.ds(pl.multiple_of(n * L, L), L)

    def body(n, carry):
        st_f, st_b = carry
        rf, rb = chunk_rows(n), chunk_rows(nc - 1 - n)
        for rows_sel, z_ref, lb_d, st, rev, dst in ((rf, zf_ref, lb_f, st_f, False, of_ref),
                                                    (rb, zb_ref, lb_b, st_b, True, ob_ref)):
            q = q_ref[0, rows_sel, :] * (A_KDIM ** -0.5)
            v_t = v_ref[0, rows_sel, :].T.astype(BF16)
            lhs, rhs_t, st_new = _gla_chunk(q, z_ref[0, rows_sel, :], v_t, lb_d, st, rev)
            dst[rows_sel, :] = lax.dot_general(lhs, rhs_t, nt, preferred_element_type=F32)
            if rev:
                st_b = st_new
            else:
                st_f = st_new
        return st_f, st_b

    zero = jnp.zeros((A_VDIM, A_KDIM), F32)
    lax.fori_loop(0, nc, body, (zero, zero))

    def finish(n, carry):
        rs = chunk_rows(n)
        o = of_ref[rs, :] + ob_ref[rs, :]
        gate = gate_ref[0, rs, :]
        o = o * _rms_scale(o) * og_ref[0]
        o_ref[0, rs, :] = (o * (gate * jax.nn.sigmoid(gate))).astype(o_ref.dtype)
        return carry

    lax.fori_loop(0, nc, finish, 0)


def _gla(proj, lb_table, out_gain, layer_e):
    b, t, _ = proj.shape
    kb = A_KDIM
    assert A_KDIM == A_VDIM and t % GLA_CHUNK == 0

    def col(group):
        return pl.BlockSpec((1, t, kb), lambda bi, h, group=group: (bi, 0, group * A_HEADS + h))

    n_tab = lb_table.shape[0]
    return pl.pallas_call(
        functools.partial(_gla_kernel, layer_e=layer_e),
        grid=(b, A_HEADS),
        in_specs=[col(0), col(1), col(2), col(3), col(4),
                  pl.BlockSpec((n_tab, 2, kb), lambda bi, h: (0, 0, h)),
                  pl.BlockSpec((1, 1, kb), lambda bi, h: (h, 0, 0))],
        out_specs=pl.BlockSpec((1, t, kb), lambda bi, h: (bi, 0, h)),
        out_shape=jax.ShapeDtypeStruct((b, t, A_WIDTH), BF16),
        scratch_shapes=[pltpu.VMEM((t, kb), F32), pltpu.VMEM((t, kb), F32)],
        compiler_params=_params("parallel", "parallel"),
        name="gla",
    )(proj, proj, proj, proj, proj, lb_table, out_gain.reshape(A_HEADS, 1, A_VDIM))


def _dft_tables(t_len):
    n = B_GDIM
    kk = (np.arange(n)[:, None] * np.arange(n)[None, :]) % n
    ang = 2.0 * np.pi * kk / n
    lane = np.concatenate([np.cos(ang), np.sin(ang)], axis=1).astype(np.float32)
    half = t_len // 2
    m = np.arange(half)[:, None]
    tt = np.arange(half)[None, :]
    tabs = []
    for k_out in (2 * m, 2 * m + 1):
        a = 2.0 * np.pi * ((k_out * tt) % t_len) / t_len
        tabs.append(np.concatenate([np.cos(a), -np.sin(a)], axis=1).astype(np.float32))
    return lane, tabs[0], tabs[1]


def _fnet_kernel(u_ref, lane_ref, even_ref, odd_ref, o_ref):
    t_len = u_ref.shape[1]
    half = t_len // 2
    lane = lane_ref[...]
    yc, ys = [], []
    for g in range(B_GROUPS):
        ug = u_ref[0, :, g * B_GDIM:(g + 1) * B_GDIM].astype(BF16)
        y = jnp.dot(ug, lane, preferred_element_type=F32)
        yc.append(y[:, :B_GDIM])
        ys.append(y[:, B_GDIM:])
    yc = jnp.concatenate(yc, axis=1)
    ys = jnp.concatenate(ys, axis=1)
    z_sum = jnp.concatenate([yc[:half] + yc[half:], ys[:half] + ys[half:]], axis=0).astype(BF16)
    z_dif = jnp.concatenate([yc[:half] - yc[half:], ys[:half] - ys[half:]], axis=0).astype(BF16)
    norm = 1.0 / np.sqrt(float(t_len) * B_GDIM)
    o_ref[0, :, :B_WIDTH] = (jnp.dot(even_ref[...], z_sum, preferred_element_type=F32) * norm).astype(o_ref.dtype)
    o_ref[0, :, B_WIDTH:] = (jnp.dot(odd_ref[...], z_dif, preferred_element_type=F32) * norm).astype(o_ref.dtype)


def _fnet(proj):
    b, t, n_all = proj.shape
    half = t // 2
    lane, even, odd = (jnp.asarray(a).astype(BF16) for a in _dft_tables(t))
    out = pl.pallas_call(
        _fnet_kernel,
        grid=(b,),
        in_specs=[
            pl.BlockSpec((1, t, B_WIDTH), lambda bi: (bi, 0, n_all // B_WIDTH - 1)),
            pl.BlockSpec((B_GDIM, 2 * B_GDIM), lambda bi: (0, 0)),
            pl.BlockSpec((half, t), lambda bi: (0, 0)),
            pl.BlockSpec((half, t), lambda bi: (0, 0)),
        ],
        out_specs=pl.BlockSpec((1, half, 2 * B_WIDTH), lambda bi: (bi, 0, 0)),
        out_shape=jax.ShapeDtypeStruct((b, half, 2 * B_WIDTH), BF16),
        compiler_params=_params("parallel"),
        name="fnet",
    )(proj, lane, even, odd)
    return out.reshape(b, t, B_WIDTH)


def _attn_kernel(sink_ref, q_ref, kp_ref, kc_ref, kn_ref, vp_ref, vc_ref, vn_ref, o_ref):
    j = pl.program_id(1)
    nb = pl.num_programs(1)
    span = QBLOCK + 2 * WINDOW
    r = lax.broadcasted_iota(jnp.int32, (QBLOCK, span), 0)
    c = lax.broadcasted_iota(jnp.int32, (QBLOCK, span), 1)
    dist = jnp.abs(r + WINDOW - c)
    c_lo = jnp.where(j > 0, 0, WINDOW)
    c_hi = jnp.where(j < nb - 1, span, WINDOW + QBLOCK)
    valid = (dist <= WINDOW) & (c >= c_lo) & (c < c_hi)
    dist_f = dist.astype(F32)
    nt = (((1,), (1,)), ((), ()))
    scale = C_HDIM ** -0.5

    outs = []
    for h in range(C_HEADS):
        kv = h // C_GROUP
        hs = slice(h * C_HDIM, (h + 1) * C_HDIM)
        ks = slice(kv * C_HDIM, (kv + 1) * C_HDIM)
        slope = 2.0 ** (-8.0 * (h + 1) / C_HEADS)
        q = q_ref[0, :, hs] * scale
        k = jnp.concatenate([kp_ref[0, :, ks], kc_ref[0, :, ks], kn_ref[0, :, ks]], axis=0)
        v = jnp.concatenate([vp_ref[0, :, ks], vc_ref[0, :, ks], vn_ref[0, :, ks]], axis=0)
        s = lax.dot_general(q, k, nt, preferred_element_type=F32)
        s = jnp.where(valid, s - slope * dist_f, NEG_BIG)
        sink = sink_ref[h]
        m = jnp.maximum(jnp.max(s, axis=-1, keepdims=True), sink)
        p = jnp.exp(s - m)
        denom = jnp.sum(p, axis=-1, keepdims=True) + jnp.exp(sink - m)
        o = jnp.dot(p.astype(BF16), v, preferred_element_type=F32)
        outs.append(o / denom)
    o_ref[0] = jnp.concatenate(outs, axis=1).astype(o_ref.dtype)


def _attention(qkv, sink):
    b, t, _ = qkv.shape
    nb = t // QBLOCK
    qw = C_HEADS * C_HDIM
    kw = C_KV * C_HDIM
    assert qw % kw == 0
    k_col, v_col = qw // kw, qw // kw + 1

    def kv_spec(col, shift):
        def index(bi, j):
            return (bi, jnp.clip(j + shift, 0, nb - 1), col)
        return pl.BlockSpec((1, QBLOCK, kw), index)

    return pl.pallas_call(
        _attn_kernel,
        grid=(b, nb),
        in_specs=[pl.BlockSpec(memory_space=pltpu.SMEM),
                  pl.BlockSpec((1, QBLOCK, qw), lambda bi, j: (bi, j, 0)),
                  kv_spec(k_col, -1), kv_spec(k_col, 0), kv_spec(k_col, 1),
                  kv_spec(v_col, -1), kv_spec(v_col, 0), kv_spec(v_col, 1)],
        out_specs=pl.BlockSpec((1, QBLOCK, qw), lambda bi, j: (bi, j, 0)),
        out_shape=jax.ShapeDtypeStruct((b, t, qw), BF16),
        compiler_params=_params("parallel", "parallel"),
        name="window_attn",
    )(sink.astype(F32), qkv, qkv, qkv, qkv, qkv, qkv, qkv)


def _tiles(m):
    tm_big = 1024 if m % 1024 == 0 else GLA_CHUNK
    tm_mid = 512 if m % 512 == 0 else GLA_CHUNK
    return tm_big, tm_mid


def kernel(x, norm_gains, ab_w_in, ab_lb_table, ab_out_gain, ab_w_out, c_w_qkv, c_sink, c_w_out,
           ffn_w1, ffn_w3, ffn_w2):
    b, t, d = x.shape
    m = b * t
    depth = norm_gains.shape[0]
    tm_big, tm_mid = _tiles(m)
    dff = ffn_w1.shape[2]
    tf = 256 if dff % 256 == 0 else 128
    xf = x.reshape(m, d)
    for layer in range(depth):
        gains = norm_gains[layer]
        if layer % 2 == 0:
            e = layer // 2
            proj = _norm_matmul(xf, gains[0], ab_w_in[e], F32, tm_big, 512).reshape(b, t, AB_IN)
            o = _gla(proj, ab_lb_table, ab_out_gain[e], e).reshape(m, A_WIDTH)
            fo = _fnet(proj).reshape(m, B_WIDTH)
            xf = _proj_norm_res([o, fo], ab_w_out[e], xf, gains[1], tm_mid)
        else:
            o_idx = layer // 2
            qkv = _norm_matmul(xf, gains[0], c_w_qkv[o_idx], BF16, tm_big, 512)
            att = _attention(qkv.reshape(b, t, -1), c_sink[o_idx]).reshape(m, C_HEADS * C_HDIM)
            xf = _proj_norm_res([att], c_w_out[o_idx], xf, gains[1], tm_mid)
        xf = _ffn(xf, gains[2], gains[3], ffn_w1[layer], ffn_w3[layer], ffn_w2[layer], tm_big, tf)
    return xf.reshape(b, t, d)
```

```python
import functools

import numpy as np
import jax
import jax.numpy as jnp
from jax import lax
from jax.experimental import pallas as pl
from jax.experimental.pallas import tpu as pltpu

F32 = jnp.float32
BF16 = jnp.bfloat16

EPS = 1e-6
A_HEADS = 4
A_KDIM = 128
A_VDIM = 128
A_FDIM = A_HEADS * A_KDIM
A_WIDTH = A_HEADS * A_VDIM
SUB = 32
SUB_SHIFT = 5
NSUB = 4
GLA_CHUNK = SUB * NSUB
B_GROUPS = 4
B_GDIM = 128
B_WIDTH = B_GROUPS * B_GDIM
AB_IN = 3 * A_FDIM + 2 * A_WIDTH + B_WIDTH
C_HEADS = 16
C_KV = 4
C_GROUP = C_HEADS // C_KV
C_HDIM = 64
WINDOW = 128
QBLOCK = 128
NEG_BIG = -1e30

V7X_VMEM_LIMIT_BYTES = 56 * 1024 * 1024
ROW_TILE = 512
SUB_ROWS = 256
COL_TILE = 256

NT_DIMS = (((1,), (1,)), ((), ()))


def _params(*semantics):
    return pltpu.CompilerParams(dimension_semantics=semantics, vmem_limit_bytes=V7X_VMEM_LIMIT_BYTES)


def _rms_scale(y):
    return lax.rsqrt(jnp.mean(y * y, axis=-1, keepdims=True) + EPS)


def _resident(block_shape, index_map):
    return pl.BlockSpec(block_shape, index_map, pipeline_mode=pl.Buffered(1))


def _row_tile(m):
    return ROW_TILE if m % ROW_TILE == 0 else GLA_CHUNK


def _norm_matmul_kernel(x_ref, g_ref, w_ref, o_ref):
    tm = x_ref.shape[0]
    n = w_ref.shape[1]
    sub = min(SUB_ROWS, tm)
    tn = 512 if n % 512 == 0 else COL_TILE
    for r in range(tm // sub):
        rs = slice(r * sub, (r + 1) * sub)
        x = x_ref[rs, :]
        hn = (x * _rms_scale(x) * g_ref[...]).astype(BF16)
        for c in range(n // tn):
            cs = slice(c * tn, (c + 1) * tn)
            o_ref[rs, cs] = jnp.dot(hn, w_ref[:, cs], preferred_element_type=F32).astype(o_ref.dtype)


def _norm_matmul(x, gain, w_all, layer, out_dtype):
    m, d = x.shape
    n = w_all.shape[2]
    tm = _row_tile(m)
    return pl.pallas_call(
        _norm_matmul_kernel,
        grid=(m // tm,),
        in_specs=[
            pl.BlockSpec((tm, d), lambda i: (i, 0)),
            pl.BlockSpec((1, d), lambda i: (0, 0)),
            _resident((None, d, n), lambda i: (layer, 0, 0)),
        ],
        out_specs=pl.BlockSpec((tm, n), lambda i: (i, 0)),
        out_shape=jax.ShapeDtypeStruct((m, n), out_dtype),
        compiler_params=_params("arbitrary"),
        name="norm_matmul",
    )(x, gain.reshape(1, d), w_all)


def _proj_norm_res_kernel(*refs, n_in):
    a_refs, w_refs = refs[:n_in], refs[n_in:2 * n_in]
    x_ref, g_ref, o_ref = refs[2 * n_in:]
    tm = x_ref.shape[0]
    sub = min(SUB_ROWS, tm)
    for r in range(tm // sub):
        rs = slice(r * sub, (r + 1) * sub)
        y = None
        for a_ref, w_ref in zip(a_refs, w_refs):
            part = jnp.dot(a_ref[rs, :], w_ref[...], preferred_element_type=F32)
            y = part if y is None else y + part
        o_ref[rs, :] = x_ref[rs, :] + y * _rms_scale(y) * g_ref[...]


def _proj_norm_res(a_list, w_all, layer, x, gain):
    m, d = x.shape
    n_in = len(a_list)
    tm = _row_tile(m)
    k_piece = a_list[0].shape[1]
    assert all(a.shape[1] == k_piece for a in a_list) and k_piece * n_in == w_all.shape[1]
    in_specs = [pl.BlockSpec((tm, k_piece), lambda i: (i, 0)) for _ in a_list]
    in_specs += [_resident((None, k_piece, d), lambda i, p=p: (layer, p, 0)) for p in range(n_in)]
    in_specs += [pl.BlockSpec((tm, d), lambda i: (i, 0)), pl.BlockSpec((1, d), lambda i: (0, 0))]
    return pl.pallas_call(
        functools.partial(_proj_norm_res_kernel, n_in=n_in),
        grid=(m // tm,),
        in_specs=in_specs,
        out_specs=pl.BlockSpec((tm, d), lambda i: (i, 0)),
        out_shape=jax.ShapeDtypeStruct((m, d), F32),
        compiler_params=_params("arbitrary"),
        name="proj_norm_res",
    )(*a_list, *([w_all] * n_in), x, gain.reshape(1, d))


def _ffn_kernel(x_ref, g_in_ref, g_out_ref, w1_ref, w3_ref, w2_ref, o_ref, a_ref):
    tm = x_ref.shape[0]
    dff = w1_ref.shape[1]
    sub = min(SUB_ROWS, tm)
    tf = COL_TILE
    for r in range(tm // sub):
        rs = slice(r * sub, (r + 1) * sub)
        x = x_ref[rs, :]
        hn = (x * _rms_scale(x) * g_in_ref[...]).astype(BF16)
        for f in range(dff // tf):
            cs = slice(f * tf, (f + 1) * tf)
            h1 = jnp.dot(hn, w1_ref[:, cs], preferred_element_type=F32)
            h3 = jnp.dot(hn, w3_ref[:, cs], preferred_element_type=F32)
            a_ref[rs, cs] = (h1 * jax.nn.sigmoid(h1) * h3).astype(BF16)
        y = jnp.dot(a_ref[rs, :], w2_ref[...], preferred_element_type=F32)
        o_ref[rs, :] = x + y * _rms_scale(y) * g_out_ref[...]


def _ffn(x, g_in, g_out, w1_all, w3_all, w2_all, layer):
    m, d = x.shape
    dff = w1_all.shape[2]
    assert dff % COL_TILE == 0
    tm = _row_tile(m)
    return pl.pallas_call(
        _ffn_kernel,
        grid=(m // tm,),
        in_specs=[
            pl.BlockSpec((tm, d), lambda i: (i, 0)),
            pl.BlockSpec((1, d), lambda i: (0, 0)),
            pl.BlockSpec((1, d), lambda i: (0, 0)),
            _resident((None, d, dff), lambda i: (layer, 0, 0)),
            _resident((None, d, dff), lambda i: (layer, 0, 0)),
            _resident((None, dff, d), lambda i: (layer, 0, 0)),
        ],
        out_specs=pl.BlockSpec((tm, d), lambda i: (i, 0)),
        out_shape=jax.ShapeDtypeStruct((m, d), F32),
        scratch_shapes=[pltpu.VMEM((tm, dff), BF16)],
        compiler_params=_params("arbitrary"),
        name="ffn",
    )(x, g_in.reshape(1, d), g_out.reshape(1, d), w1_all, w3_all, w2_all)


def _sub_blocks(a):
    return [a[i * SUB:(i + 1) * SUB] for i in range(NSUB)]


def _gla_chunk(q, z, v_t, lb, st, rev):
    L, K = q.shape
    f = lb + (1.0 - lb) * jax.nn.sigmoid(z)
    k = 1.0 - f
    lf = jnp.log(f)

    row = lax.broadcasted_iota(jnp.int32, (L, K), 0)
    ri = row & (SUB - 1)
    g = lf
    s = 1
    while s < SUB:
        if rev:
            g = g + jnp.where(ri < SUB - s, pltpu.roll(g, L - s, axis=0), 0.0)
        else:
            g = g + jnp.where(ri >= s, pltpu.roll(g, s, axis=0), 0.0)
        s *= 2

    order = list(range(NSUB))[::-1] if rev else list(range(NSUB))
    last, mid = (0, SUB // 2) if rev else (SUB - 1, SUB // 2 - 1)
    g_p = [_sub_blocks(g)[i] for i in order]
    q_p = [_sub_blocks(q)[i] for i in order]
    k_p = [_sub_blocks(k)[i] for i in order]
    tot = [gp[last:last + 1] for gp in g_p]
    ref = [gp[mid:mid + 1] for gp in g_p]

    qr = [q_p[p] * jnp.exp(g_p[p] - ref[p]) for p in range(NSUB)]
    kr = [k_p[p] * jnp.exp(ref[p] - g_p[p]) for p in range(NSUB)]
    qs = [q_p[p] * jnp.exp(g_p[p]) for p in range(NSUB)]
    ko = [k_p[p] * jnp.exp(tot[p] - g_p[p]) for p in range(NSUB)]
    e1, e2 = jnp.exp(tot[1]), jnp.exp(tot[2])
    e3 = jnp.exp(tot[3])
    qs2 = [qs[0], qs[1], qs[2], qs[3] * e2]
    ko2 = [ko[0] * e1, ko[1], ko[2], ko[3]]
    e23 = e2 * e3
    q_in = [qs[0], qs[1] * jnp.exp(tot[0]), qs[2] * jnp.exp(tot[0] + tot[1]),
            qs[3] * jnp.exp(tot[0] + tot[1] + tot[2])]
    k_out = [ko[0] * (e1 * e23), ko[1] * e23, ko[2] * e3, ko[3]]
    d_all = jnp.exp(tot[0] + tot[1] + tot[2] + tot[3])

    def rows(pieces):
        by_index = [None] * NSUB
        for p, i in enumerate(order):
            by_index[i] = pieces[p]
        return jnp.concatenate(by_index, axis=0).astype(BF16)

    x_diag = lax.dot_general(rows(qr), rows(kr), NT_DIMS, preferred_element_type=F32)
    x_adj = lax.dot_general(rows(qs), rows(ko), NT_DIMS, preferred_element_type=F32)
    x_half = lax.dot_general(rows(qs2), rows(ko2), NT_DIMS, preferred_element_type=F32)

    r = lax.broadcasted_iota(jnp.int32, (L, L), 0)
    c = lax.broadcasted_iota(jnp.int32, (L, L), 1)
    if rev:
        r, c = (L - 1) - r, (L - 1) - c
    bi, bj = r >> SUB_SHIFT, c >> SUB_SHIFT
    m_diag = (bi == bj) & (c <= r)
    m_adj = (bi == bj + 1) & ((bi & 1) == 1)
    m_half = (bi >= NSUB // 2) & (bj < NSUB // 2)
    a = jnp.where(m_diag, x_diag, jnp.where(m_adj, x_adj, jnp.where(m_half, x_half, 0.0)))

    lhs = jnp.concatenate([a.astype(BF16), rows(q_in)], axis=1)
    rhs_t = jnp.concatenate([v_t, st.astype(BF16)], axis=1)
    st_new = st * d_all + jnp.dot(v_t, rows(k_out), preferred_element_type=F32)
    return lhs, rhs_t, st_new


def _gla_kernel(q_ref, zf_ref, zb_ref, v_ref, gate_ref, lbt_ref, og_ref, o_ref, of_ref, ob_ref, *, layer_e):
    t_len = q_ref.shape[1]
    L = GLA_CHUNK
    nc = t_len // L

    tab = lbt_ref[...]
    ex = jnp.exp(tab - jnp.max(tab, axis=0, keepdims=True))
    sm = ex / jnp.sum(ex, axis=0, keepdims=True)
    lb = sm[0]
    for i in range(1, layer_e + 1):
        lb = lb + sm[i]
    lb_f, lb_b = lb[0:1], lb[1:2]

    def chunk_rows(n):
        return pl.ds(pl.multiple_of(n * L, L), L)

    def body(n, carry):
        st_f, st_b = carry
        rf, rb = chunk_rows(n), chunk_rows(nc - 1 - n)
        for rows_sel, z_ref, lb_d, st, rev, dst in ((rf, zf_ref, lb_f, st_f, False, of_ref),
                                                    (rb, zb_ref, lb_b, st_b, True, ob_ref)):
            q = q_ref[0, rows_sel, :].astype(F32) * (A_KDIM ** -0.5)
            v_t = v_ref[0, rows_sel, :].astype(F32).T.astype(BF16)
            z = z_ref[0, rows_sel, :].astype(F32)
            lhs, rhs_t, st_new = _gla_chunk(q, z, v_t, lb_d, st, rev)
            dst[rows_sel, :] = lax.dot_general(lhs, rhs_t, NT_DIMS, preferred_element_type=F32)
            if rev:
                st_b = st_new
            else:
                st_f = st_new
        return st_f, st_b

    zero = jnp.zeros((A_VDIM, A_KDIM), F32)
    lax.fori_loop(0, nc, body, (zero, zero))

    def finish(n, carry):
        rs = chunk_rows(n)
        o = of_ref[rs, :] + ob_ref[rs, :]
        gate = gate_ref[0, rs, :].astype(F32)
        o = o * _rms_scale(o) * og_ref[0]
        o_ref[0, rs, :] = (o * (gate * jax.nn.sigmoid(gate))).astype(o_ref.dtype)
        return carry

    lax.fori_loop(0, nc, finish, 0)


def _gla(proj, lb_table, out_gain, layer_e):
    b, t, _ = proj.shape
    kb = A_KDIM
    assert A_KDIM == A_VDIM and t % GLA_CHUNK == 0

    def col(group):
        return pl.BlockSpec((1, t, kb), lambda bi, h, group=group: (bi, 0, group * A_HEADS + h))

    n_tab = lb_table.shape[0]
    return pl.pallas_call(
        functools.partial(_gla_kernel, layer_e=layer_e),
        grid=(b, A_HEADS),
        in_specs=[col(0), col(1), col(2), col(3), col(4),
                  pl.BlockSpec((n_tab, 2, kb), lambda bi, h: (0, 0, h)),
                  pl.BlockSpec((1, 1, kb), lambda bi, h: (h, 0, 0))],
        out_specs=pl.BlockSpec((1, t, kb), lambda bi, h: (bi, 0, h)),
        out_shape=jax.ShapeDtypeStruct((b, t, A_WIDTH), BF16),
        scratch_shapes=[pltpu.VMEM((t, kb), F32), pltpu.VMEM((t, kb), F32)],
        compiler_params=_params("parallel", "parallel"),
        name="gla",
    )(proj, proj, proj, proj, proj, lb_table, out_gain.reshape(A_HEADS, 1, A_VDIM))


def _dft_tables(t_len):
    n = B_GDIM
    kk = (np.arange(n)[:, None] * np.arange(n)[None, :]) % n
    ang = 2.0 * np.pi * kk / n
    lane = np.concatenate([np.cos(ang), np.sin(ang)], axis=1).astype(np.float32)
    half = t_len // 2
    m = np.arange(half)[:, None]
    tt = np.arange(half)[None, :]
    tabs = []
    for k_out in (2 * m, 2 * m + 1):
        a = 2.0 * np.pi * ((k_out * tt) % t_len) / t_len
        tabs.append(np.concatenate([np.cos(a), -np.sin(a)], axis=1).astype(np.float32))
    return lane, tabs[0], tabs[1]


def _fnet_kernel(u_ref, lane_ref, even_ref, odd_ref, o_ref):
    t_len = u_ref.shape[1]
    half = t_len // 2
    lane = lane_ref[...]
    yc, ys = [], []
    for g in range(B_GROUPS):
        ug = u_ref[0, :, g * B_GDIM:(g + 1) * B_GDIM].astype(BF16)
        y = jnp.dot(ug, lane, preferred_element_type=F32)
        yc.append(y[:, :B_GDIM])
        ys.append(y[:, B_GDIM:])
    yc = jnp.concatenate(yc, axis=1)
    ys = jnp.concatenate(ys, axis=1)
    z_sum = jnp.concatenate([yc[:half] + yc[half:], ys[:half] + ys[half:]], axis=0).astype(BF16)
    z_dif = jnp.concatenate([yc[:half] - yc[half:], ys[:half] - ys[half:]], axis=0).astype(BF16)
    norm = 1.0 / np.sqrt(float(t_len) * B_GDIM)
    o_ref[0, :, :B_WIDTH] = (jnp.dot(even_ref[...], z_sum, preferred_element_type=F32) * norm).astype(o_ref.dtype)
    o_ref[0, :, B_WIDTH:] = (jnp.dot(odd_ref[...], z_dif, preferred_element_type=F32) * norm).astype(o_ref.dtype)


def _fnet(proj):
    b, t, n_all = proj.shape
    half = t // 2
    lane, even, odd = (jnp.asarray(a).astype(BF16) for a in _dft_tables(t))
    out = pl.pallas_call(
        _fnet_kernel,
        grid=(b,),
        in_specs=[
            pl.BlockSpec((1, t, B_WIDTH), lambda bi: (bi, 0, n_all // B_WIDTH - 1)),
            _resident((B_GDIM, 2 * B_GDIM), lambda bi: (0, 0)),
            _resident((half, t), lambda bi: (0, 0)),
            _resident((half, t), lambda bi: (0, 0)),
        ],
        out_specs=pl.BlockSpec((1, half, 2 * B_WIDTH), lambda bi: (bi, 0, 0)),
        out_shape=jax.ShapeDtypeStruct((b, half, 2 * B_WIDTH), BF16),
        compiler_params=_params("arbitrary"),
        name="fnet",
    )(proj, lane, even, odd)
    return out.reshape(b, t, B_WIDTH)


def _attn_kernel(sink_ref, q_ref, kp_ref, kc_ref, kn_ref, vp_ref, vc_ref, vn_ref, o_ref, bias_ref):
    bi = pl.program_id(0)
    j = pl.program_id(1)
    nb = pl.num_programs(1)
    span = QBLOCK + 2 * WINDOW

    @pl.when((bi == 0) & (j == 0))
    def _():
        c = lax.broadcasted_iota(jnp.int32, (span, QBLOCK), 0)
        r = lax.broadcasted_iota(jnp.int32, (span, QBLOCK), 1)
        dist = jnp.abs(r + WINDOW - c)
        dist_f = dist.astype(F32)
        for variant, (c_lo, c_hi) in enumerate(((WINDOW, span), (0, span), (0, WINDOW + QBLOCK))):
            valid = (dist <= WINDOW) & (c >= c_lo) & (c < c_hi)
            for h in range(C_HEADS):
                slope = 2.0 ** (-8.0 * (h + 1) / C_HEADS)
                kv, g = divmod(h, C_GROUP)
                bias_ref[variant, kv, :, g * QBLOCK:(g + 1) * QBLOCK] = jnp.where(valid, -slope * dist_f, NEG_BIG)

    variant = jnp.where(j == 0, 0, jnp.where(j == nb - 1, 2, 1))
    scale = C_HDIM ** -0.5
    tn_dims = (((0,), (0,)), ((), ()))
    outs = []
    for kv in range(C_KV):
        ks = slice(kv * C_HDIM, (kv + 1) * C_HDIM)
        k = jnp.concatenate([kp_ref[0, :, ks], kc_ref[0, :, ks], kn_ref[0, :, ks]], axis=0)
        v = jnp.concatenate([vp_ref[0, :, ks], vc_ref[0, :, ks], vn_ref[0, :, ks]], axis=0)
        heads = [kv * C_GROUP + g for g in range(C_GROUP)]
        q = jnp.concatenate([q_ref[0, :, h * C_HDIM:(h + 1) * C_HDIM] for h in heads], axis=0) * scale
        s = lax.dot_general(k, q, NT_DIMS, preferred_element_type=F32)
        s = s + bias_ref[variant, kv]
        sink = jnp.concatenate([jnp.full((1, QBLOCK), sink_ref[h], F32) for h in heads], axis=1)
        m = jnp.maximum(jnp.max(s, axis=0, keepdims=True), sink)
        p = jnp.exp(s - m)
        denom = jnp.sum(p, axis=0, keepdims=True) + jnp.exp(sink - m)
        o_t = lax.dot_general(v, p.astype(BF16), tn_dims, preferred_element_type=F32)
        o_t = o_t / denom
        outs += [o_t[:, g * QBLOCK:(g + 1) * QBLOCK].T for g in range(C_GROUP)]
    o_ref[0] = jnp.concatenate(outs, axis=1).astype(o_ref.dtype)


def _attention(qkv, sink):
    b, t, _ = qkv.shape
    nb = t // QBLOCK
    assert nb >= 2 and WINDOW == QBLOCK
    qw = C_HEADS * C_HDIM
    kw = C_KV * C_HDIM
    assert qw % kw == 0
    k_col, v_col = qw // kw, qw // kw + 1
    span = QBLOCK + 2 * WINDOW

    def kv_spec(col, shift):
        def index(bi, j):
            return (bi, jnp.clip(j + shift, 0, nb - 1), col)
        return pl.BlockSpec((1, QBLOCK, kw), index)

    return pl.pallas_call(
        _attn_kernel,
        grid=(b, nb),
        in_specs=[pl.BlockSpec(memory_space=pltpu.SMEM),
                  pl.BlockSpec((1, QBLOCK, qw), lambda bi, j: (bi, j, 0)),
                  kv_spec(k_col, -1), kv_spec(k_col, 0), kv_spec(k_col, 1),
                  kv_spec(v_col, -1), kv_spec(v_col, 0), kv_spec(v_col, 1)],
        out_specs=pl.BlockSpec((1, QBLOCK, qw), lambda bi, j: (bi, j, 0)),
        out_shape=jax.ShapeDtypeStruct((b, t, qw), BF16),
        scratch_shapes=[pltpu.VMEM((3, C_KV, span, C_GROUP * QBLOCK), F32)],
        compiler_params=_params("arbitrary", "arbitrary"),
        name="window_attn",
    )(sink.astype(F32), qkv, qkv, qkv, qkv, qkv, qkv, qkv)


def kernel(x, norm_gains, ab_w_in, ab_lb_table, ab_out_gain, ab_w_out, c_w_qkv, c_sink, c_w_out,
           ffn_w1, ffn_w3, ffn_w2):
    b, t, d = x.shape
    m = b * t
    depth = norm_gains.shape[0]
    ab_w_in, ab_w_out, c_w_qkv, c_w_out, ffn_w1, ffn_w3, ffn_w2 = (
        w.astype(BF16) for w in (ab_w_in, ab_w_out, c_w_qkv, c_w_out, ffn_w1, ffn_w3, ffn_w2))
    xf = x.reshape(m, d)
    for layer in range(depth):
        gains = norm_gains[layer]
        if layer % 2 == 0:
            e = layer // 2
            proj = _norm_matmul(xf, gains[0], ab_w_in, e, BF16).reshape(b, t, AB_IN)
            o = _gla(proj, ab_lb_table, ab_out_gain[e], e).reshape(m, A_WIDTH)
            fo = _fnet(proj).reshape(m, B_WIDTH)
            xf = _proj_norm_res([o, fo], ab_w_out, e, xf, gains[1])
        else:
            o_idx = layer // 2
            qkv = _norm_matmul(xf, gains[0], c_w_qkv, o_idx, BF16)
            att = _attention(qkv.reshape(b, t, -1), c_sink[o_idx]).reshape(m, C_HEADS * C_HDIM)
            xf = _proj_norm_res([att], c_w_out, o_idx, xf, gains[1])
        xf = _ffn(xf, gains[2], gains[3], ffn_w1, ffn_w3, ffn_w2, layer)
    return xf.reshape(b, t, d)
```

```python
import functools

import numpy as np
import jax
import jax.numpy as jnp
from jax import lax
from jax.experimental import pallas as pl
from jax.experimental.pallas import tpu as pltpu

F32 = jnp.float32
BF16 = jnp.bfloat16

EPS = 1e-6
A_HEADS = 4
A_KDIM = 128
A_VDIM = 128
A_FDIM = A_HEADS * A_KDIM
A_WIDTH = A_HEADS * A_VDIM
SUB = 32
SUB_SHIFT = 5
NSUB = 4
GLA_CHUNK = SUB * NSUB
CHUNKS_PER_STEP = 8
EMIT_CHUNKS_PER_STEP = 8
B_GROUPS = 4
B_GDIM = 128
B_WIDTH = B_GROUPS * B_GDIM
AB_IN = 3 * A_FDIM + 2 * A_WIDTH + B_WIDTH
C_HEADS = 16
C_KV = 4
C_GROUP = C_HEADS // C_KV
C_HDIM = 64
WINDOW = 128
QBLOCK = 128
LOG2_E = 1.4426950408889634
NEG_BIG = -1e30

V7X_VMEM_LIMIT_BYTES = 58 * 1024 * 1024
ROW_TILE = 512
SUB_ROWS = 256
COL_TILE = 256

NT_DIMS = (((1,), (1,)), ((), ()))


def _params(*semantics):
    return pltpu.CompilerParams(dimension_semantics=semantics, vmem_limit_bytes=V7X_VMEM_LIMIT_BYTES)


def _rms_scale(y):
    return lax.rsqrt(jnp.mean(y * y, axis=-1, keepdims=True) + EPS)


def _resident(block_shape, index_map):
    return pl.BlockSpec(block_shape, index_map, pipeline_mode=pl.Buffered(1))


def _row_tile(m, want=ROW_TILE):
    return want if m % want == 0 else (ROW_TILE if m % ROW_TILE == 0 else GLA_CHUNK)


def _cast_weight_once(w_ref, wb_ref):
    @pl.when(pl.program_id(0) == 0)
    def _():
        rows = w_ref.shape[0]
        step = min(rows, 256)
        for r0 in range(0, rows, step):
            wb_ref[r0:r0 + step, :] = w_ref[r0:r0 + step, :].astype(BF16)


def _norm_matmul_kernel(x_ref, g_ref, w_ref, o_ref, wb_ref):
    _cast_weight_once(w_ref, wb_ref)
    tm = x_ref.shape[0]
    n = w_ref.shape[1]
    sub = min(SUB_ROWS, tm)
    tn = 512 if n % 512 == 0 else COL_TILE
    hns = []
    for r in range(tm // sub):
        x = x_ref[r * sub:(r + 1) * sub, :]
        hns.append((x * _rms_scale(x) * g_ref[...]).astype(BF16))
    for r, hn in enumerate(hns):
        rs = slice(r * sub, (r + 1) * sub)
        for c in range(n // tn):
            cs = slice(c * tn, (c + 1) * tn)
            o_ref[rs, cs] = jnp.dot(hn, wb_ref[:, cs], preferred_element_type=F32).astype(o_ref.dtype)


def _norm_matmul(x, gain, w_all, layer, out_dtype):
    m, d = x.shape
    n = w_all.shape[2]
    tm = _row_tile(m, 2 * ROW_TILE)
    return pl.pallas_call(
        _norm_matmul_kernel,
        grid=(m // tm,),
        in_specs=[
            pl.BlockSpec((tm, d), lambda i: (i, 0)),
            pl.BlockSpec((1, d), lambda i: (0, 0)),
            _resident((None, d, n), lambda i: (layer, 0, 0)),
        ],
        out_specs=pl.BlockSpec((tm, n), lambda i: (i, 0)),
        out_shape=jax.ShapeDtypeStruct((m, n), out_dtype),
        scratch_shapes=[pltpu.VMEM((d, n), BF16)],
        compiler_params=_params("arbitrary"),
        name="norm_matmul",
    )(x, gain.reshape(1, d), w_all)


def _in_proj_kernel(x_ref, g_ref, w_ref, lbt_ref, o_ref, lf_ref, wb_ref, *, layer_e):
    _cast_weight_once(w_ref, wb_ref)
    tm = x_ref.shape[0]
    sub = min(SUB_ROWS, tm)
    tn = A_FDIM

    tab = lbt_ref[...]
    ex = jnp.exp(tab - jnp.max(tab, axis=0, keepdims=True))
    sm = ex / jnp.sum(ex, axis=0, keepdims=True)
    lb = sm[0]
    for i in range(1, layer_e + 1):
        lb = lb + sm[i]

    hns = []
    for r in range(tm // sub):
        x = x_ref[r * sub:(r + 1) * sub, :]
        hns.append((x * _rms_scale(x) * g_ref[...]).astype(BF16))
    gate_cols = (1, 2)
    z = {(r, c): jnp.dot(hn, wb_ref[:, c * tn:(c + 1) * tn], preferred_element_type=F32)
         for r, hn in enumerate(hns) for c in gate_cols}
    for (r, c), y in z.items():
        rs = slice(r * sub, (r + 1) * sub)
        lb_d = lb[c - 1:c]
        f = lb_d + (1.0 - lb_d) * jax.nn.sigmoid(y)
        lf = jnp.log(f)
        hi = lf.astype(BF16)
        lf_ref[rs, (c - 1) * tn:c * tn] = hi
        lf_ref[rs, (c + 1) * tn:(c + 2) * tn] = (lf - hi.astype(F32)).astype(BF16)
        o_ref[rs, c * tn:(c + 1) * tn] = (1.0 - f).astype(o_ref.dtype)
    for r, hn in enumerate(hns):
        rs = slice(r * sub, (r + 1) * sub)
        for c in range(AB_IN // tn):
            if c not in gate_cols:
                cs = slice(c * tn, (c + 1) * tn)
                y = jnp.dot(hn, wb_ref[:, cs], preferred_element_type=F32)
                if c == 0:
                    y = y * (A_KDIM ** -0.5)
                elif c == 4:
                    y = y * jax.nn.sigmoid(y)
                o_ref[rs, cs] = y.astype(o_ref.dtype)


def _in_proj(x, gain, w_all, layer_e, lb_table):
    m, d = x.shape
    assert w_all.shape[2] == AB_IN and AB_IN % A_FDIM == 0
    tm = _row_tile(m, 2 * ROW_TILE)
    return pl.pallas_call(
        functools.partial(_in_proj_kernel, layer_e=layer_e),
        grid=(m // tm,),
        in_specs=[
            pl.BlockSpec((tm, d), lambda i: (i, 0)),
            pl.BlockSpec((1, d), lambda i: (0, 0)),
            _resident((None, d, AB_IN), lambda i: (layer_e, 0, 0)),
            pl.BlockSpec(lb_table.shape, lambda i: (0, 0, 0)),
        ],
        out_specs=[pl.BlockSpec((tm, AB_IN), lambda i: (i, 0)), pl.BlockSpec((tm, 4 * A_FDIM), lambda i: (i, 0))],
        out_shape=[jax.ShapeDtypeStruct((m, AB_IN), BF16), jax.ShapeDtypeStruct((m, 4 * A_FDIM), BF16)],
        scratch_shapes=[pltpu.VMEM((d, AB_IN), BF16)],
        compiler_params=_params("arbitrary"),
        name="in_proj",
    )(x, gain.reshape(1, d), w_all, lb_table)


def _mix_ffn_items(load_mixed, wo_refs, x_ref, gains_ref, w1_ref, w3_ref, w2_ref, o_ref, act_ref):
    tm = x_ref.shape[0]
    dff = w1_ref.shape[1]
    sub = min(SUB_ROWS, tm)
    tf = COL_TILE
    g_mix, g_in, g_out = gains_ref[1:2, :], gains_ref[2:3, :], gains_ref[3:4, :]
    per_sub = []
    for r in range(tm // sub):
        rs = slice(r * sub, (r + 1) * sub)

        def head(rs=rs):
            y = None
            for p, wo_ref in enumerate(wo_refs):
                part = jnp.dot(load_mixed(p, rs), wo_ref[...], preferred_element_type=F32)
                y = part if y is None else y + part
            x1 = x_ref[rs, :] + y * _rms_scale(y) * g_mix
            o_ref[rs, :] = x1
            act_ref[rs, dff:] = (x1 * _rms_scale(x1) * g_in).astype(BF16)

        def tile(f, rs=rs):
            cs = slice(f * tf, (f + 1) * tf)
            hn = act_ref[rs, dff:]
            h1 = jnp.dot(hn, w1_ref[:, cs], preferred_element_type=F32)
            h3 = jnp.dot(hn, w3_ref[:, cs], preferred_element_type=F32)
            act_ref[rs, cs] = (h1 * jax.nn.sigmoid(h1) * h3).astype(BF16)

        def tail(rs=rs):
            y2 = jnp.dot(act_ref[rs, :dff], w2_ref[...], preferred_element_type=F32)
            o_ref[rs, :] = o_ref[rs, :] + y2 * _rms_scale(y2) * g_out

        per_sub.append((head, [functools.partial(tile, f) for f in range(dff // tf)], tail))
    items = [per_sub[0][0]]
    for r, (_, tiles, tail) in enumerate(per_sub):
        items += tiles[:-2]
        if r + 1 < len(per_sub):
            items.append(per_sub[r + 1][0])
        items += tiles[-2:] + [tail]
    return items


FFN_STAGE_UP = (8, 32)
FFN_STAGE_DOWN = (4, 176)


def _ffn_weight_scratch(d, dff):
    (up_slots, up_rows), (down_slots, down_rows) = FFN_STAGE_UP, FFN_STAGE_DOWN
    assert d % up_rows == 0 and dff % down_rows == 0 and up_rows % 16 == 0 and down_rows % 16 == 0
    return [pltpu.VMEM((d, dff), BF16), pltpu.VMEM((d, dff), BF16), pltpu.VMEM((dff, d), BF16),
            pltpu.VMEM((up_slots, up_rows, dff), F32), pltpu.VMEM((down_slots, down_rows, d), F32),
            pltpu.SemaphoreType.DMA((up_slots,)), pltpu.SemaphoreType.DMA((down_slots,))]


def _load_ffn_weights_once(layer, w_hbm_refs, wb_refs, stage_up, stage_down, sem_up, sem_down):
    @pl.when(pl.program_id(0) == 0)
    def _():
        jobs, used = [], {}
        for w_hbm, wb, stage, sem in zip(w_hbm_refs, wb_refs, (stage_up, stage_up, stage_down),
                                         (sem_up, sem_up, sem_down)):
            slots, rows = stage.shape[0], stage.shape[1]
            for r0 in range(0, wb.shape[0], rows):
                n_before = used.get(id(stage), 0)
                used[id(stage)] = n_before + 1
                slot = n_before % slots
                prev = [j for j, job in enumerate(jobs) if job[2] is stage and job[5] == slot]
                jobs.append((w_hbm, wb, stage, sem, r0, slot, rows, prev[-1] if prev else -1))

        def copy(j):
            w_hbm, _, stage, sem, r0, slot, rows, _ = jobs[j]
            return pltpu.make_async_copy(w_hbm.at[layer, pl.ds(r0, rows), :], stage.at[slot], sem.at[slot])

        started = 0
        for j, (_, wb, stage, _, r0, slot, rows, _) in enumerate(jobs):
            while started < len(jobs) and jobs[started][7] < j:
                copy(started).start()
                started += 1
            copy(j).wait()
            wb[r0:r0 + rows, :] = stage[slot].astype(BF16)


def _mix_ffn_kernel(*refs, n_in, layer):
    a_refs, wo_refs = refs[:n_in], refs[n_in:2 * n_in]
    x_ref, gains_ref, w1_hbm, w3_hbm, w2_hbm, o_ref, act_ref = refs[2 * n_in:2 * n_in + 7]
    wob_refs = refs[2 * n_in + 7:3 * n_in + 7]
    w1b, w3b, w2b, stage_up, stage_down, sem_up, sem_down = refs[3 * n_in + 7:]
    for wo_ref, wob_ref in zip(wo_refs, wob_refs):
        _cast_weight_once(wo_ref, wob_ref)
    _load_ffn_weights_once(layer, (w1_hbm, w3_hbm, w2_hbm), (w1b, w3b, w2b), stage_up, stage_down, sem_up, sem_down)
    for item in _mix_ffn_items(lambda p, rs: a_refs[p][rs, :], wob_refs,
                               x_ref, gains_ref, w1b, w3b, w2b, o_ref, act_ref):
        item()


def _mix_ffn(a_list, wo_all, wo_layer, x, gains, w1_all, w3_all, w2_all, layer):
    m, d = x.shape
    n_in = len(a_list)
    dff = w1_all.shape[2]
    tm = _row_tile(m, 2 * ROW_TILE)
    k_piece = a_list[0].shape[1]
    assert all(a.shape[1] == k_piece for a in a_list) and k_piece * n_in == wo_all.shape[1]
    assert dff % COL_TILE == 0
    in_specs = [pl.BlockSpec((tm, k_piece), lambda i: (i, 0)) for _ in a_list]
    in_specs += [_resident((None, k_piece, d), lambda i, p=p: (wo_layer, p, 0)) for p in range(n_in)]
    in_specs += [
        pl.BlockSpec((tm, d), lambda i: (i, 0)),
        pl.BlockSpec(gains.shape, lambda i: (0, 0)),
        pl.BlockSpec(memory_space=pl.ANY), pl.BlockSpec(memory_space=pl.ANY), pl.BlockSpec(memory_space=pl.ANY),
    ]
    return pl.pallas_call(
        functools.partial(_mix_ffn_kernel, n_in=n_in, layer=layer),
        grid=(m // tm,),
        in_specs=in_specs,
        out_specs=pl.BlockSpec((tm, d), lambda i: (i, 0)),
        out_shape=jax.ShapeDtypeStruct((m, d), F32),
        scratch_shapes=([pltpu.VMEM((tm, dff + d), BF16)] + [pltpu.VMEM((k_piece, d), BF16)] * n_in
                        + _ffn_weight_scratch(d, dff)),
        compiler_params=_params("arbitrary"),
        name="mix_ffn",
    )(*a_list, *([wo_all] * n_in), x, gains, w1_all, w3_all, w2_all)


def _sub_blocks(a):
    return [a[i * SUB:(i + 1) * SUB] for i in range(NSUB)]


def _gla_running_log(lf_hi, lf_lo, tri2):
    return jnp.dot(tri2, jnp.concatenate([lf_hi, lf_lo], axis=0), preferred_element_type=F32)


def _gla_scores(q, k, g, v_t, rev):
    L, K = q.shape
    order = list(range(NSUB))[::-1] if rev else list(range(NSUB))
    last, mid = (0, SUB // 2) if rev else (SUB - 1, SUB // 2 - 1)
    g_p = [_sub_blocks(g)[i] for i in order]
    q_p = [_sub_blocks(q)[i] for i in order]
    k_p = [_sub_blocks(k)[i] for i in order]
    tot = [gp[last:last + 1] for gp in g_p]
    ref = [gp[mid:mid + 1] for gp in g_p]

    qr = [q_p[p] * jnp.exp(g_p[p] - ref[p]) for p in range(NSUB)]
    kr = [k_p[p] * jnp.exp(ref[p] - g_p[p]) for p in range(NSUB)]
    qs = [q_p[p] * jnp.exp(g_p[p]) for p in range(NSUB)]
    ko = [k_p[p] * jnp.exp(tot[p] - g_p[p]) for p in range(NSUB)]
    e1, e2, e3 = jnp.exp(tot[1]), jnp.exp(tot[2]), jnp.exp(tot[3])
    e23 = e2 * e3
    q_in = [qs[0], qs[1] * jnp.exp(tot[0]), qs[2] * jnp.exp(tot[0] + tot[1]),
            qs[3] * jnp.exp(tot[0] + tot[1] + tot[2])]
    k_out = [ko[0] * (e1 * e23), ko[1] * e23, ko[2] * e3, ko[3]]
    d_all = jnp.exp(tot[0] + tot[1] + tot[2] + tot[3])

    zero = jnp.zeros((SUB, K), F32)

    def rows(pieces):
        by_index = [None] * NSUB
        for p, i in enumerate(order):
            by_index[i] = pieces[p]
        return jnp.concatenate(by_index, axis=0).astype(BF16)

    lhs_off = jnp.concatenate([rows([zero, qs[1], zero, zero]),
                               rows([zero, zero, zero, qs[3]]),
                               rows([zero, zero, qs[2], qs[3] * e2])], axis=1)
    rhs_off = jnp.concatenate([rows([ko[0], zero, zero, zero]),
                               rows([zero, zero, ko[2], zero]),
                               rows([ko[0] * e1, ko[1], zero, zero])], axis=1)
    x_diag = lax.dot_general(rows(qr), rows(kr), NT_DIMS, preferred_element_type=F32)
    x_off = lax.dot_general(lhs_off, rhs_off, NT_DIMS, preferred_element_type=F32)
    kv = jnp.dot(v_t, rows(k_out), preferred_element_type=F32)
    return x_diag, x_off, rows(q_in), kv, d_all


def _gla_kernel(q_ref, kf_ref, kb_ref, v_ref, gate_ref, hif_ref, hib_ref, lof_ref, lob_ref, og_ref, o_ref,
                lhs_ref, kv_ref, dall_ref, stb_ref, vt_ref, tri_ref, diag_ref):
    t_len = q_ref.shape[1]
    L = GLA_CHUNK
    nc = t_len // L

    r = lax.broadcasted_iota(jnp.int32, (L, L), 0)
    c = lax.broadcasted_iota(jnp.int32, (L, L), 1)
    same = (r >> SUB_SHIFT) == (c >> SUB_SHIFT)
    for d, keep in enumerate((same & (c <= r), same & (c >= r))):
        ones = jnp.where(keep, 1.0, 0.0)
        diag_ref[d] = ones
        tri_ref[d] = jnp.concatenate([ones, ones], axis=1).astype(BF16)

    def chunk_rows(n):
        return pl.ds(pl.multiple_of(n * L, L), L)

    local_chunks = min(CHUNKS_PER_STEP, nc)
    emit_chunks = min(EMIT_CHUNKS_PER_STEP, nc)
    assert nc % local_chunks == 0 and nc % emit_chunks == 0

    def local_part(i, carry):
        chunks = [i * local_chunks + u for u in range(local_chunks)]
        jobs = [(u, d) for u in range(local_chunks) for d in (0, 1)]
        k_refs, hi_refs, lo_refs = (kf_ref, kb_ref), (hif_ref, hib_ref), (lof_ref, lob_ref)
        gates = []
        for u, d in jobs:
            rs = chunk_rows(chunks[u])
            g = _gla_running_log(hi_refs[d][0, rs, :], lo_refs[d][0, rs, :], tri_ref[d])
            gates.append((k_refs[d][0, rs, :].astype(F32), g))
        v_ts, qs = [], []
        for n in chunks:
            rs = chunk_rows(n)
            v_ts.append(v_ref[0, rs, :].astype(F32).T.astype(BF16))
            vt_ref[n] = v_ts[-1]
            qs.append(q_ref[0, rs, :].astype(F32))
        scores = [_gla_scores(qs[u], k, g, v_ts[u], rev=bool(d)) for (u, d), (k, g) in zip(jobs, gates)]
        for u, n in enumerate(chunks):
            a_sum, q_ins = None, []
            for (uu, d), (x_diag, x_off, q_in, kv, d_all) in zip(jobs, scores):
                if uu != u:
                    continue
                a = jnp.where(diag_ref[d] != 0.0, x_diag, x_off)
                a_sum = a if a_sum is None else a_sum + a
                q_ins.append(q_in)
                kv_ref[d, n] = kv
                dall_ref[d, n] = d_all
            lhs_ref[n] = jnp.concatenate([a_sum.astype(BF16)] + q_ins, axis=1)
        return carry

    lax.fori_loop(0, nc // local_chunks, local_part, 0)

    zero = jnp.zeros((A_VDIM, A_KDIM), F32)

    def backward_state(i, st):
        n = nc - 1 - i
        stb_ref[n] = st.astype(BF16)
        return st * dall_ref[1, n] + kv_ref[1, n]

    lax.fori_loop(0, nc, backward_state, zero)

    ones = jnp.ones((2 * A_VDIM, A_VDIM), BF16)

    def emit(i, st):
        chunks = [i * emit_chunks + u for u in range(emit_chunks)]
        outs = []
        for n in chunks:
            rhs_t = jnp.concatenate([vt_ref[n], st.astype(BF16), stb_ref[n]], axis=1)
            outs.append(lax.dot_general(lhs_ref[n], rhs_t, NT_DIMS, preferred_element_type=F32))
            st = st * dall_ref[0, n] + kv_ref[0, n]
        sqs = []
        for o in outs:
            o2 = o * o
            hi = o2.astype(BF16)
            lo = (o2 - hi.astype(F32)).astype(BF16)
            sqs.append(jnp.dot(jnp.concatenate([hi, lo], axis=1), ones, preferred_element_type=F32))
        for n, o, sq in zip(chunks, outs, sqs):
            rs = chunk_rows(n)
            o = o * lax.rsqrt(sq * (1.0 / A_VDIM) + EPS) * og_ref[0]
            o_ref[0, rs, :] = (o * gate_ref[0, rs, :].astype(F32)).astype(o_ref.dtype)
        return st

    lax.fori_loop(0, nc // emit_chunks, emit, zero)


def _gla(proj, lf, out_gain):
    b, t, _ = proj.shape
    kb = A_KDIM
    L = GLA_CHUNK
    assert A_KDIM == A_VDIM == L and t % (2 * L) == 0
    nc = t // L

    def col(group):
        return pl.BlockSpec((1, t, kb), lambda bi, h, group=group: (bi, 0, group * A_HEADS + h))

    return pl.pallas_call(
        _gla_kernel,
        grid=(b, A_HEADS),
        in_specs=[col(0), col(1), col(2), col(3), col(4), col(0), col(1), col(2), col(3),
                  pl.BlockSpec((1, 1, kb), lambda bi, h: (h, 0, 0))],
        out_specs=pl.BlockSpec((1, t, kb), lambda bi, h: (bi, 0, h)),
        out_shape=jax.ShapeDtypeStruct((b, t, A_WIDTH), BF16),
        scratch_shapes=[pltpu.VMEM((nc, L, 3 * L), BF16),
                        pltpu.VMEM((2, nc, A_VDIM, A_KDIM), F32),
                        pltpu.VMEM((2, nc, 1, A_KDIM), F32),
                        pltpu.VMEM((nc, A_VDIM, A_KDIM), BF16),
                        pltpu.VMEM((nc, A_VDIM, L), BF16),
                        pltpu.VMEM((2, L, 2 * L), BF16), pltpu.VMEM((2, L, L), F32)],
        compiler_params=_params("parallel", "parallel"),
        name="gla",
    )(proj, proj, proj, proj, proj, lf, lf, lf, lf, out_gain.reshape(A_HEADS, 1, A_VDIM))


def _dft_tables(t_len):
    n = B_GDIM
    kk = (np.arange(n)[:, None] * np.arange(n)[None, :]) % n
    ang = 2.0 * np.pi * kk / n
    lane = np.concatenate([np.cos(ang), np.sin(ang)], axis=1).astype(np.float32)
    half = t_len // 2
    t_out = np.arange(half)[:, None]
    nn = np.arange(half)[None, :]
    tabs = []
    for t_in in (2 * nn, 2 * nn + 1):
        a = 2.0 * np.pi * ((t_in * t_out) % t_len) / t_len
        tabs.append(np.concatenate([np.cos(a), -np.sin(a)], axis=1).astype(np.float32))
    return lane, tabs[0], tabs[1]


def _fnet_kernel(u_ref, lane_ref, even_ref, odd_ref, o_ref, y_ref):
    t_len = u_ref.shape[1]
    half = t_len // 2
    lane = lane_ref[...]
    for g in range(B_GROUPS):
        ug = u_ref[0, :, g * B_GDIM:(g + 1) * B_GDIM].astype(BF16)
        y = jnp.dot(ug, lane, preferred_element_type=F32)
        y_ref[2 * g] = y[:, :B_GDIM]
        y_ref[2 * g + 1] = y[:, B_GDIM:]

    def rows_of_parity(parity):
        sel = pl.ds(parity, half, stride=2)
        parts = [jnp.concatenate([y_ref[2 * g + cs, sel, :] for g in range(B_GROUPS)], axis=1) for cs in (0, 1)]
        return jnp.concatenate(parts, axis=0).astype(BF16)

    norm = 1.0 / np.sqrt(float(t_len) * B_GDIM)
    e = jnp.dot(even_ref[...], rows_of_parity(0), preferred_element_type=F32)
    o = jnp.dot(odd_ref[...], rows_of_parity(1), preferred_element_type=F32)
    o_ref[0, :half, :] = ((e + o) * norm).astype(o_ref.dtype)
    o_ref[0, half:, :] = ((e - o) * norm).astype(o_ref.dtype)


def _fnet(proj):
    b, t, n_all = proj.shape
    half = t // 2
    lane, even, odd = (jnp.asarray(a).astype(BF16) for a in _dft_tables(t))
    return pl.pallas_call(
        _fnet_kernel,
        grid=(b,),
        in_specs=[
            pl.BlockSpec((1, t, B_WIDTH), lambda bi: (bi, 0, n_all // B_WIDTH - 1)),
            _resident((B_GDIM, 2 * B_GDIM), lambda bi: (0, 0)),
            _resident((half, t), lambda bi: (0, 0)),
            _resident((half, t), lambda bi: (0, 0)),
        ],
        out_specs=pl.BlockSpec((1, t, B_WIDTH), lambda bi: (bi, 0, 0)),
        out_shape=jax.ShapeDtypeStruct((b, t, B_WIDTH), BF16),
        scratch_shapes=[pltpu.VMEM((2 * B_GROUPS, t, B_GDIM), F32)],
        compiler_params=_params("arbitrary"),
        name="fnet",
    )(proj, lane, even, odd)


ATTN_ITEMS_AFTER_LAST_OUTPUT = 3
ATTN_SPAN = QBLOCK + 2 * WINDOW
ATTN_GROUP_HEADS = [[kv * C_GROUP + g for g in range(C_GROUP)] for kv in range(C_KV)]


def _attn_write_bias_tables(bias_ref):
    c = lax.broadcasted_iota(jnp.int32, (ATTN_SPAN, QBLOCK), 0)
    r = lax.broadcasted_iota(jnp.int32, (ATTN_SPAN, QBLOCK), 1)
    dist = jnp.abs(r + WINDOW - c)
    dist_f = dist.astype(F32)
    for variant, (c_lo, c_hi) in enumerate(((WINDOW, ATTN_SPAN), (0, ATTN_SPAN), (0, WINDOW + QBLOCK))):
        valid = (dist <= WINDOW) & (c >= c_lo) & (c < c_hi)
        bias_ref[variant] = jnp.where(valid, -LOG2_E * dist_f, NEG_BIG)


def _attn_block_stages(q_blk, k_span, v_span, variant, sink_ref, bias_ref, store):
    scale = C_HDIM ** -0.5 * LOG2_E
    tn_dims = (((0,), (0,)), ((), ()))
    held = {}

    def logits():
        held["s"] = []
        for kv in range(C_KV):
            q = jnp.concatenate([q_blk(h) for h in ATTN_GROUP_HEADS[kv]], axis=0)
            q = (q.astype(F32) * scale).astype(BF16)
            held["s"].append(lax.dot_general(k_span(kv), q, NT_DIMS, preferred_element_type=F32))

    def softmax():
        held["p"], held["denom"] = [], []
        for kv in range(C_KV):
            base = bias_ref[variant]
            slopes = [2.0 ** (-8.0 * (h + 1) / C_HEADS) for h in ATTN_GROUP_HEADS[kv]]
            s = held["s"][kv] + jnp.concatenate([slope * base for slope in slopes], axis=1)
            sink = jnp.concatenate([jnp.full((1, QBLOCK), sink_ref[h] * LOG2_E, F32)
                                    for h in ATTN_GROUP_HEADS[kv]], axis=1)
            m = jnp.maximum(jnp.max(s, axis=0, keepdims=True), sink)
            p = jnp.exp2(s - m)
            held["denom"].append(jnp.sum(p, axis=0, keepdims=True) + jnp.exp2(sink - m))
            held["p"].append(p.astype(BF16))

    def output():
        outs_t = [lax.dot_general(v_span(kv), held["p"][kv], tn_dims, preferred_element_type=F32)
                  for kv in range(C_KV)]
        outs = []
        for kv in range(C_KV):
            o_t = outs_t[kv] / held["denom"][kv]
            outs += [o_t[:, g * QBLOCK:(g + 1) * QBLOCK].T for g in range(C_GROUP)]
        store(jnp.concatenate(outs, axis=1))

    return logits, softmax, output


def _attn_ffn_kernel(sink_ref, q_ref, kp_ref, km_ref, kn_ref, vp_ref, vm_ref, vn_ref,
                     wo_ref, x_ref, gains_ref, w1_hbm, w3_hbm, w2_hbm, o_ref,
                     bias_ref, att_ref, act_ref, wob_ref, w1b, w3b, w2b, stage_up, stage_down, sem_up, sem_down,
                     *, blocks_per_seq, layer):
    s = pl.program_id(0)
    tm = q_ref.shape[0]
    nq = tm // QBLOCK
    slot_attn = s % 2
    slot_ffn = (s + 1) % 2

    @pl.when(s == 0)
    def _():
        _attn_write_bias_tables(bias_ref)

    _cast_weight_once(wo_ref, wob_ref)
    _load_ffn_weights_once(layer, (w1_hbm, w3_hbm, w2_hbm), (w1b, w3b, w2b), stage_up, stage_down, sem_up, sem_down)

    seq_pos = jnp.minimum(s, pl.num_programs(0) - 2) % blocks_per_seq
    first = jnp.where(seq_pos == 0, 0, 1)
    last = jnp.where(seq_pos == blocks_per_seq - 1, 2, 1)

    def key_piece(prev_ref, main_ref, next_ref, i, cols):
        if i == 0:
            return prev_ref[:, cols]
        if i == nq + 1:
            return next_ref[:, cols]
        return main_ref[(i - 1) * QBLOCK:i * QBLOCK, cols]

    def span_of(prev_ref, main_ref, next_ref, jq):
        def get(kv):
            cols = slice(kv * C_HDIM, (kv + 1) * C_HDIM)
            return jnp.concatenate([key_piece(prev_ref, main_ref, next_ref, jq + i, cols) for i in range(3)], axis=0)
        return get

    def attention_stages():
        stages = []
        for jq in range(nq):
            rows = slice(jq * QBLOCK, (jq + 1) * QBLOCK)
            variant = first if jq == 0 else (last if jq == nq - 1 else 1)

            def store(o, rows=rows):
                att_ref[slot_attn, rows, :] = o.astype(att_ref.dtype)

            stages.append(_attn_block_stages(
                lambda h, rows=rows: q_ref[rows, h * C_HDIM:(h + 1) * C_HDIM],
                span_of(kp_ref, km_ref, kn_ref, jq), span_of(vp_ref, vm_ref, vn_ref, jq),
                variant, sink_ref, bias_ref, store))
        return stages

    def mixer_items():
        return _mix_ffn_items(lambda p, rs: att_ref[slot_ffn, rs, :], [wob_ref],
                              x_ref, gains_ref, w1b, w3b, w2b, o_ref, act_ref)

    last_step = pl.num_programs(0) - 1

    @pl.when(s == 0)
    def _():
        for logits, softmax, output in attention_stages():
            logits()
            softmax()
            output()

    @pl.when(s == last_step)
    def _():
        for item in mixer_items():
            item()

    @pl.when((s > 0) & (s < last_step))
    def _():
        stages, items = attention_stages(), mixer_items()
        share = -(-len(items) // nq)
        pending_output = None
        for jq, (logits, softmax, output) in enumerate(stages):
            logits()
            if pending_output is not None:
                pending_output()
            softmax()
            mine = items[jq * share:(jq + 1) * share]
            after = ATTN_ITEMS_AFTER_LAST_OUTPUT if jq == nq - 1 else 0
            for item in mine[:len(mine) - after]:
                item()
            if jq == nq - 1:
                output()
            else:
                pending_output = output
            for item in mine[len(mine) - after:]:
                item()


def _attn_ffn(qkv, sink, t, wo_all, wo_layer, x, gains, w1_all, w3_all, w2_all, layer):
    m, d = x.shape
    dff = w1_all.shape[2]
    tm = _row_tile(m)
    assert WINDOW == QBLOCK and tm % QBLOCK == 0 and t % tm == 0 and t // QBLOCK >= 2 and dff % COL_TILE == 0
    qw = C_HEADS * C_HDIM
    kw = C_KV * C_HDIM
    assert qw % kw == 0 and wo_all.shape[1] == qw
    k_col, v_col = qw // kw, qw // kw + 1
    n = m // tm
    per = tm // QBLOCK
    n_qb = m // QBLOCK

    def attn_block(s):
        return jnp.minimum(s, n - 1)

    def main_spec(width, col):
        return pl.BlockSpec((tm, width), lambda s: (attn_block(s), col))

    def edge_spec(col, after):
        def index(s):
            qb = attn_block(s) * per + (per if after else -1)
            return (jnp.clip(qb, 0, n_qb - 1), col)
        return pl.BlockSpec((QBLOCK, kw), index)

    def ffn_rows(s):
        return (jnp.maximum(s - 1, 0), 0)

    return pl.pallas_call(
        functools.partial(_attn_ffn_kernel, blocks_per_seq=t // tm, layer=layer),
        grid=(n + 1,),
        in_specs=[pl.BlockSpec(memory_space=pltpu.SMEM),
                  main_spec(qw, 0),
                  edge_spec(k_col, False), main_spec(kw, k_col), edge_spec(k_col, True),
                  edge_spec(v_col, False), main_spec(kw, v_col), edge_spec(v_col, True),
                  _resident((None, qw, d), lambda s: (wo_layer, 0, 0)),
                  pl.BlockSpec((tm, d), ffn_rows),
                  pl.BlockSpec(gains.shape, lambda s: (0, 0)),
                  pl.BlockSpec(memory_space=pl.ANY), pl.BlockSpec(memory_space=pl.ANY),
                  pl.BlockSpec(memory_space=pl.ANY)],
        out_specs=pl.BlockSpec((tm, d), ffn_rows),
        out_shape=jax.ShapeDtypeStruct((m, d), F32),
        scratch_shapes=[pltpu.VMEM((3, ATTN_SPAN, QBLOCK), F32),
                        pltpu.VMEM((2, tm, qw), BF16),
                        pltpu.VMEM((tm, dff + d), BF16),
                        pltpu.VMEM((qw, d), BF16)] + _ffn_weight_scratch(d, dff),
        compiler_params=_params("arbitrary"),
        name="attn_ffn",
    )(sink.astype(F32), qkv, qkv, qkv, qkv, qkv, qkv, qkv, wo_all, x, gains, w1_all, w3_all, w2_all)


def kernel(x, norm_gains, ab_w_in, ab_lb_table, ab_out_gain, ab_w_out, c_w_qkv, c_sink, c_w_out,
           ffn_w1, ffn_w3, ffn_w2):
    b, t, d = x.shape
    m = b * t
    depth = norm_gains.shape[0]
    xf = x.reshape(m, d)
    for layer in range(depth):
        gains = norm_gains[layer]
        if layer % 2 == 0:
            e = layer // 2
            proj, lf = _in_proj(xf, gains[0], ab_w_in, e, ab_lb_table)
            proj = proj.reshape(b, t, AB_IN)
            o = _gla(proj, lf.reshape(b, t, 4 * A_FDIM), ab_out_gain[e]).reshape(m, A_WIDTH)
            fo = _fnet(proj).reshape(m, B_WIDTH)
            xf = _mix_ffn([o, fo], ab_w_out, e, xf, gains, ffn_w1, ffn_w3, ffn_w2, layer)
        else:
            w_idx = layer // 2
            qkv = _norm_matmul(xf, gains[0], c_w_qkv, w_idx, BF16)
            xf = _attn_ffn(qkv, c_sink[w_idx], t, c_w_out, w_idx, xf, gains, ffn_w1, ffn_w3, ffn_w2, layer)
    return xf.reshape(b, t, d)
```

```python
import functools

import numpy as np
import jax
import jax.numpy as jnp
from jax import lax
from jax.experimental import pallas as pl
from jax.experimental.pallas import tpu as pltpu

F32 = jnp.float32
BF16 = jnp.bfloat16

EPS = 1e-6
A_HEADS = 4
A_KDIM = 128
A_VDIM = 128
A_FDIM = A_HEADS * A_KDIM
A_WIDTH = A_HEADS * A_VDIM
SUB = 32
SUB_SHIFT = 5
NSUB = 4
GLA_CHUNK = SUB * NSUB
CHUNKS_PER_STEP = 8
EMIT_CHUNKS_PER_STEP = 16
B_GROUPS = 4
B_GDIM = 128
B_WIDTH = B_GROUPS * B_GDIM
AB_IN = 3 * A_FDIM + 2 * A_WIDTH + B_WIDTH
C_HEADS = 16
C_KV = 4
C_GROUP = C_HEADS // C_KV
C_HDIM = 64
WINDOW = 128
QBLOCK = 128
LOG2_E = 1.4426950408889634
NEG_BIG = -1e30

V7X_VMEM_LIMIT_BYTES = 58 * 1024 * 1024
ROW_TILE = 512
SUB_ROWS = 256
COL_TILE = 256

NT_DIMS = (((1,), (1,)), ((), ()))


def _params(*semantics):
    return pltpu.CompilerParams(dimension_semantics=semantics, vmem_limit_bytes=V7X_VMEM_LIMIT_BYTES)


def _rms_scale(y):
    return lax.rsqrt(jnp.mean(y * y, axis=-1, keepdims=True) + EPS)


def _resident(block_shape, index_map):
    return pl.BlockSpec(block_shape, index_map, pipeline_mode=pl.Buffered(1))


def _row_tile(m, want=ROW_TILE):
    return want if m % want == 0 else (ROW_TILE if m % ROW_TILE == 0 else GLA_CHUNK)


def _cast_weight_once(w_ref, wb_ref):
    @pl.when(pl.program_id(0) == 0)
    def _():
        rows = w_ref.shape[0]
        step = min(rows, 256)
        for r0 in range(0, rows, step):
            wb_ref[r0:r0 + step, :] = w_ref[r0:r0 + step, :].astype(BF16)


def _norm_matmul_kernel(x_ref, g_ref, w_ref, o_ref, wb_ref):
    _cast_weight_once(w_ref, wb_ref)
    tm = x_ref.shape[0]
    n = w_ref.shape[1]
    sub = min(SUB_ROWS, tm)
    tn = 512 if n % 512 == 0 else COL_TILE
    hns = []
    for r in range(tm // sub):
        x = x_ref[r * sub:(r + 1) * sub, :]
        hns.append((x * _rms_scale(x) * g_ref[...]).astype(BF16))
    for r, hn in enumerate(hns):
        rs = slice(r * sub, (r + 1) * sub)
        for c in range(n // tn):
            cs = slice(c * tn, (c + 1) * tn)
            o_ref[rs, cs] = jnp.dot(hn, wb_ref[:, cs], preferred_element_type=F32).astype(o_ref.dtype)


def _norm_matmul(x, gain, w_all, layer, out_dtype):
    m, d = x.shape
    n = w_all.shape[2]
    tm = _row_tile(m, 2 * ROW_TILE)
    return pl.pallas_call(
        _norm_matmul_kernel,
        grid=(m // tm,),
        in_specs=[
            pl.BlockSpec((tm, d), lambda i: (i, 0)),
            pl.BlockSpec((1, d), lambda i: (0, 0)),
            _resident((None, d, n), lambda i: (layer, 0, 0)),
        ],
        out_specs=pl.BlockSpec((tm, n), lambda i: (i, 0)),
        out_shape=jax.ShapeDtypeStruct((m, n), out_dtype),
        scratch_shapes=[pltpu.VMEM((d, n), BF16)],
        compiler_params=_params("arbitrary"),
        name="norm_matmul",
    )(x, gain.reshape(1, d), w_all)


def _in_proj_kernel(x_ref, g_ref, w_ref, lbt_ref, o_ref, lf_ref, wb_ref, *, layer_e):
    _cast_weight_once(w_ref, wb_ref)
    tm = x_ref.shape[0]
    sub = min(SUB_ROWS, tm)
    tn = A_FDIM

    tab = lbt_ref[...]
    ex = jnp.exp(tab - jnp.max(tab, axis=0, keepdims=True))
    sm = ex / jnp.sum(ex, axis=0, keepdims=True)
    lb = sm[0]
    for i in range(1, layer_e + 1):
        lb = lb + sm[i]

    hns = []
    for r in range(tm // sub):
        x = x_ref[r * sub:(r + 1) * sub, :]
        hns.append((x * _rms_scale(x) * g_ref[...]).astype(BF16))
    gate_cols = (1, 2)
    z = {(r, c): jnp.dot(hn, wb_ref[:, c * tn:(c + 1) * tn], preferred_element_type=F32)
         for r, hn in enumerate(hns) for c in gate_cols}
    for (r, c), y in z.items():
        rs = slice(r * sub, (r + 1) * sub)
        lb_d = lb[c - 1:c]
        f = lb_d + (1.0 - lb_d) * jax.nn.sigmoid(y)
        lf = jnp.log(f)
        hi = lf.astype(BF16)
        lf_ref[rs, (c - 1) * tn:c * tn] = hi
        lf_ref[rs, (c + 1) * tn:(c + 2) * tn] = (lf - hi.astype(F32)).astype(BF16)
        o_ref[rs, c * tn:(c + 1) * tn] = (1.0 - f).astype(o_ref.dtype)
    for r, hn in enumerate(hns):
        rs = slice(r * sub, (r + 1) * sub)
        for c in range(AB_IN // tn):
            if c not in gate_cols:
                cs = slice(c * tn, (c + 1) * tn)
                y = jnp.dot(hn, wb_ref[:, cs], preferred_element_type=F32)
                if c == 0:
                    y = y * (A_KDIM ** -0.5)
                elif c == 4:
                    y = y * jax.nn.sigmoid(y)
                o_ref[rs, cs] = y.astype(o_ref.dtype)


def _in_proj(x, gain, w_all, layer_e, lb_table):
    m, d = x.shape
    assert w_all.shape[2] == AB_IN and AB_IN % A_FDIM == 0
    tm = _row_tile(m, 2 * ROW_TILE)
    return pl.pallas_call(
        functools.partial(_in_proj_kernel, layer_e=layer_e),
        grid=(m // tm,),
        in_specs=[
            pl.BlockSpec((tm, d), lambda i: (i, 0)),
            pl.BlockSpec((1, d), lambda i: (0, 0)),
            _resident((None, d, AB_IN), lambda i: (layer_e, 0, 0)),
            pl.BlockSpec(lb_table.shape, lambda i: (0, 0, 0)),
        ],
        out_specs=[pl.BlockSpec((tm, AB_IN), lambda i: (i, 0)), pl.BlockSpec((tm, 4 * A_FDIM), lambda i: (i, 0))],
        out_shape=[jax.ShapeDtypeStruct((m, AB_IN), BF16), jax.ShapeDtypeStruct((m, 4 * A_FDIM), BF16)],
        scratch_shapes=[pltpu.VMEM((d, AB_IN), BF16)],
        compiler_params=_params("arbitrary"),
        name="in_proj",
    )(x, gain.reshape(1, d), w_all, lb_table)


def _mix_ffn_items(load_mixed, wo_refs, x_ref, gains_ref, w1_ref, w3_ref, w2_ref, o_ref, act_ref):
    tm = x_ref.shape[0]
    dff = w1_ref.shape[1]
    sub = min(SUB_ROWS, tm)
    tf = COL_TILE
    g_mix, g_in, g_out = gains_ref[1:2, :], gains_ref[2:3, :], gains_ref[3:4, :]
    per_sub = []
    for r in range(tm // sub):
        rs = slice(r * sub, (r + 1) * sub)

        def head(rs=rs):
            y = None
            for p, wo_ref in enumerate(wo_refs):
                part = jnp.dot(load_mixed(p, rs), wo_ref[...], preferred_element_type=F32)
                y = part if y is None else y + part
            x1 = x_ref[rs, :] + y * _rms_scale(y) * g_mix
            o_ref[rs, :] = x1
            act_ref[rs, dff:] = (x1 * _rms_scale(x1) * g_in).astype(BF16)

        def tile(f, rs=rs):
            cs = slice(f * tf, (f + 1) * tf)
            hn = act_ref[rs, dff:]
            h1 = jnp.dot(hn, w1_ref[:, cs], preferred_element_type=F32)
            h3 = jnp.dot(hn, w3_ref[:, cs], preferred_element_type=F32)
            act_ref[rs, cs] = (h1 * jax.nn.sigmoid(h1) * h3).astype(BF16)

        def tail(rs=rs):
            y2 = jnp.dot(act_ref[rs, :dff], w2_ref[...], preferred_element_type=F32)
            o_ref[rs, :] = o_ref[rs, :] + y2 * _rms_scale(y2) * g_out

        per_sub.append((head, [functools.partial(tile, f) for f in range(dff // tf)], tail))
    items = [per_sub[0][0]]
    for r, (_, tiles, tail) in enumerate(per_sub):
        items += tiles[:-2]
        if r + 1 < len(per_sub):
            items.append(per_sub[r + 1][0])
        items += tiles[-2:] + [tail]
    return items


FFN_STAGE_UP = (8, 32)
FFN_STAGE_DOWN = (4, 176)


def _ffn_weight_scratch(d, dff):
    (up_slots, up_rows), (down_slots, down_rows) = FFN_STAGE_UP, FFN_STAGE_DOWN
    assert d % up_rows == 0 and dff % down_rows == 0 and up_rows % 16 == 0 and down_rows % 16 == 0
    return [pltpu.VMEM((d, dff), BF16), pltpu.VMEM((d, dff), BF16), pltpu.VMEM((dff, d), BF16),
            pltpu.VMEM((up_slots, up_rows, dff), F32), pltpu.VMEM((down_slots, down_rows, d), F32),
            pltpu.SemaphoreType.DMA((up_slots,)), pltpu.SemaphoreType.DMA((down_slots,))]


def _load_ffn_weights_once(layer, w_hbm_refs, wb_refs, stage_up, stage_down, sem_up, sem_down):
    @pl.when(pl.program_id(0) == 0)
    def _():
        jobs, used = [], {}
        for w_hbm, wb, stage, sem in zip(w_hbm_refs, wb_refs, (stage_up, stage_up, stage_down),
                                         (sem_up, sem_up, sem_down)):
            slots, rows = stage.shape[0], stage.shape[1]
            for r0 in range(0, wb.shape[0], rows):
                n_before = used.get(id(stage), 0)
                used[id(stage)] = n_before + 1
                slot = n_before % slots
                prev = [j for j, job in enumerate(jobs) if job[2] is stage and job[5] == slot]
                jobs.append((w_hbm, wb, stage, sem, r0, slot, rows, prev[-1] if prev else -1))

        def copy(j):
            w_hbm, _, stage, sem, r0, slot, rows, _ = jobs[j]
            return pltpu.make_async_copy(w_hbm.at[layer, pl.ds(r0, rows), :], stage.at[slot], sem.at[slot])

        started = 0
        for j, (_, wb, stage, _, r0, slot, rows, _) in enumerate(jobs):
            while started < len(jobs) and jobs[started][7] < j:
                copy(started).start()
                started += 1
            copy(j).wait()
            wb[r0:r0 + rows, :] = stage[slot].astype(BF16)


def _mix_ffn_kernel(*refs, n_in, layer):
    a_refs, wo_refs = refs[:n_in], refs[n_in:2 * n_in]
    x_ref, gains_ref, w1_hbm, w3_hbm, w2_hbm, o_ref, act_ref = refs[2 * n_in:2 * n_in + 7]
    wob_refs = refs[2 * n_in + 7:3 * n_in + 7]
    w1b, w3b, w2b, stage_up, stage_down, sem_up, sem_down = refs[3 * n_in + 7:]
    for wo_ref, wob_ref in zip(wo_refs, wob_refs):
        _cast_weight_once(wo_ref, wob_ref)
    _load_ffn_weights_once(layer, (w1_hbm, w3_hbm, w2_hbm), (w1b, w3b, w2b), stage_up, stage_down, sem_up, sem_down)
    for item in _mix_ffn_items(lambda p, rs: a_refs[p][rs, :], wob_refs,
                               x_ref, gains_ref, w1b, w3b, w2b, o_ref, act_ref):
        item()


def _mix_ffn(a_list, wo_all, wo_layer, x, gains, w1_all, w3_all, w2_all, layer):
    m, d = x.shape
    n_in = len(a_list)
    dff = w1_all.shape[2]
    tm = _row_tile(m, 2 * ROW_TILE)
    k_piece = a_list[0].shape[1]
    assert all(a.shape[1] == k_piece for a in a_list) and k_piece * n_in == wo_all.shape[1]
    assert dff % COL_TILE == 0
    in_specs = [pl.BlockSpec((tm, k_piece), lambda i: (i, 0)) for _ in a_list]
    in_specs += [_resident((None, k_piece, d), lambda i, p=p: (wo_layer, p, 0)) for p in range(n_in)]
    in_specs += [
        pl.BlockSpec((tm, d), lambda i: (i, 0)),
        pl.BlockSpec(gains.shape, lambda i: (0, 0)),
        pl.BlockSpec(memory_space=pl.ANY), pl.BlockSpec(memory_space=pl.ANY), pl.BlockSpec(memory_space=pl.ANY),
    ]
    return pl.pallas_call(
        functools.partial(_mix_ffn_kernel, n_in=n_in, layer=layer),
        grid=(m // tm,),
        in_specs=in_specs,
        out_specs=pl.BlockSpec((tm, d), lambda i: (i, 0)),
        out_shape=jax.ShapeDtypeStruct((m, d), F32),
        scratch_shapes=([pltpu.VMEM((tm, dff + d), BF16)] + [pltpu.VMEM((k_piece, d), BF16)] * n_in
                        + _ffn_weight_scratch(d, dff)),
        compiler_params=_params("arbitrary"),
        name="mix_ffn",
    )(*a_list, *([wo_all] * n_in), x, gains, w1_all, w3_all, w2_all)


def _sub_blocks(a):
    return [a[i * SUB:(i + 1) * SUB] for i in range(NSUB)]


def _gla_running_log(lf_hi, lf_lo, tri2):
    return jnp.dot(tri2, jnp.concatenate([lf_hi, lf_lo], axis=0), preferred_element_type=F32)


def _gla_scores(q, k, g, v_t, rev):
    L, K = q.shape
    order = list(range(NSUB))[::-1] if rev else list(range(NSUB))
    last, mid = (0, SUB // 2) if rev else (SUB - 1, SUB // 2 - 1)
    g_p = [_sub_blocks(g)[i] for i in order]
    q_p = [_sub_blocks(q)[i] for i in order]
    k_p = [_sub_blocks(k)[i] for i in order]
    tot = [gp[last:last + 1] for gp in g_p]
    ref = [gp[mid:mid + 1] for gp in g_p]

    qr = [q_p[p] * jnp.exp(g_p[p] - ref[p]) for p in range(NSUB)]
    kr = [k_p[p] * jnp.exp(ref[p] - g_p[p]) for p in range(NSUB)]
    qs = [q_p[p] * jnp.exp(g_p[p]) for p in range(NSUB)]
    ko = [k_p[p] * jnp.exp(tot[p] - g_p[p]) for p in range(NSUB)]
    e1, e2, e3 = jnp.exp(tot[1]), jnp.exp(tot[2]), jnp.exp(tot[3])
    e23 = e2 * e3
    q_in = [qs[0], qs[1] * jnp.exp(tot[0]), qs[2] * jnp.exp(tot[0] + tot[1]),
            qs[3] * jnp.exp(tot[0] + tot[1] + tot[2])]
    k_out = [ko[0] * (e1 * e23), ko[1] * e23, ko[2] * e3, ko[3]]
    d_all = jnp.exp(tot[0] + tot[1] + tot[2] + tot[3])

    zero = jnp.zeros((SUB, K), F32)

    def rows(pieces):
        by_index = [None] * NSUB
        for p, i in enumerate(order):
            by_index[i] = pieces[p]
        return jnp.concatenate(by_index, axis=0).astype(BF16)

    lhs_off = jnp.concatenate([rows([zero, qs[1], zero, zero]),
                               rows([zero, zero, zero, qs[3]]),
                               rows([zero, zero, qs[2], qs[3] * e2])], axis=1)
    rhs_off = jnp.concatenate([rows([ko[0], zero, zero, zero]),
                               rows([zero, zero, ko[2], zero]),
                               rows([ko[0] * e1, ko[1], zero, zero])], axis=1)
    x_diag = lax.dot_general(rows(qr), rows(kr), NT_DIMS, preferred_element_type=F32)
    x_off = lax.dot_general(lhs_off, rhs_off, NT_DIMS, preferred_element_type=F32)
    kv = jnp.dot(v_t, rows(k_out), preferred_element_type=F32)
    return x_diag, x_off, rows(q_in), kv, d_all


def _gla_kernel(q_ref, kf_ref, kb_ref, v_ref, gate_ref, hif_ref, hib_ref, lof_ref, lob_ref, og_ref, o_ref,
                lhs_ref, kv_ref, dall_ref, stb_ref, vt_ref, tri_ref, diag_ref):
    t_len = q_ref.shape[1]
    L = GLA_CHUNK
    nc = t_len // L

    r = lax.broadcasted_iota(jnp.int32, (L, L), 0)
    c = lax.broadcasted_iota(jnp.int32, (L, L), 1)
    same = (r >> SUB_SHIFT) == (c >> SUB_SHIFT)
    for d, keep in enumerate((same & (c <= r), same & (c >= r))):
        ones = jnp.where(keep, 1.0, 0.0)
        diag_ref[d] = ones
        tri_ref[d] = jnp.concatenate([ones, ones], axis=1).astype(BF16)

    def chunk_rows(n):
        return pl.ds(pl.multiple_of(n * L, L), L)

    local_chunks = min(CHUNKS_PER_STEP, nc)
    emit_chunks = min(EMIT_CHUNKS_PER_STEP, nc)
    assert nc % local_chunks == 0 and nc % emit_chunks == 0

    def local_part(i, carry):
        chunks = [i * local_chunks + u for u in range(local_chunks)]
        jobs = [(u, d) for u in range(local_chunks) for d in (0, 1)]
        k_refs, hi_refs, lo_refs = (kf_ref, kb_ref), (hif_ref, hib_ref), (lof_ref, lob_ref)
        gates = []
        for u, d in jobs:
            rs = chunk_rows(chunks[u])
            g = _gla_running_log(hi_refs[d][0, rs, :], lo_refs[d][0, rs, :], tri_ref[d])
            gates.append((k_refs[d][0, rs, :].astype(F32), g))
        v_ts, qs = [], []
        for n in chunks:
            rs = chunk_rows(n)
            v_ts.append(v_ref[0, rs, :].astype(F32).T.astype(BF16))
            vt_ref[n] = v_ts[-1]
            qs.append(q_ref[0, rs, :].astype(F32))
        scores = [_gla_scores(qs[u], k, g, v_ts[u], rev=bool(d)) for (u, d), (k, g) in zip(jobs, gates)]
        for u, n in enumerate(chunks):
            a_sum, q_ins = None, []
            for (uu, d), (x_diag, x_off, q_in, kv, d_all) in zip(jobs, scores):
                if uu != u:
                    continue
                a = jnp.where(diag_ref[d] != 0.0, x_diag, x_off)
                a_sum = a if a_sum is None else a_sum + a
                q_ins.append(q_in)
                kv_ref[d, n] = kv
                dall_ref[d, n] = d_all
            lhs_ref[n] = jnp.concatenate([a_sum.astype(BF16)] + q_ins, axis=1)
        return carry

    lax.fori_loop(0, nc // local_chunks, local_part, 0)

    zero = jnp.zeros((A_VDIM, A_KDIM), F32)

    def backward_state(i, st):
        n = nc - 1 - i
        stb_ref[n] = st.astype(BF16)
        return st * dall_ref[1, n] + kv_ref[1, n]

    lax.fori_loop(0, nc, backward_state, zero)

    ones = jnp.ones((2 * A_VDIM, A_VDIM), BF16)

    def emit(i, st):
        chunks = [i * emit_chunks + u for u in range(emit_chunks)]
        outs = []
        for n in chunks:
            rhs_t = jnp.concatenate([vt_ref[n], st.astype(BF16), stb_ref[n]], axis=1)
            outs.append(lax.dot_general(lhs_ref[n], rhs_t, NT_DIMS, preferred_element_type=F32))
            st = st * dall_ref[0, n] + kv_ref[0, n]
        sqs = []
        for o in outs:
            o2 = o * o
            hi = o2.astype(BF16)
            lo = (o2 - hi.astype(F32)).astype(BF16)
            sqs.append(jnp.dot(jnp.concatenate([hi, lo], axis=1), ones, preferred_element_type=F32))
        for n, o, sq in zip(chunks, outs, sqs):
            rs = chunk_rows(n)
            o = o * lax.rsqrt(sq * (1.0 / A_VDIM) + EPS) * og_ref[0]
            o_ref[0, rs, :] = (o * gate_ref[0, rs, :].astype(F32)).astype(o_ref.dtype)
        return st

    lax.fori_loop(0, nc // emit_chunks, emit, zero)


def _gla(proj, lf, out_gain):
    b, t, _ = proj.shape
    kb = A_KDIM
    L = GLA_CHUNK
    assert A_KDIM == A_VDIM == L and t % (2 * L) == 0
    nc = t // L

    def col(group):
        return pl.BlockSpec((1, t, kb), lambda bi, h, group=group: (bi, 0, group * A_HEADS + h))

    return pl.pallas_call(
        _gla_kernel,
        grid=(b, A_HEADS),
        in_specs=[col(0), col(1), col(2), col(3), col(4), col(0), col(1), col(2), col(3),
                  pl.BlockSpec((1, 1, kb), lambda bi, h: (h, 0, 0))],
        out_specs=pl.BlockSpec((1, t, kb), lambda bi, h: (bi, 0, h)),
        out_shape=jax.ShapeDtypeStruct((b, t, A_WIDTH), BF16),
        scratch_shapes=[pltpu.VMEM((nc, L, 3 * L), BF16),
                        pltpu.VMEM((2, nc, A_VDIM, A_KDIM), F32),
                        pltpu.VMEM((2, nc, 1, A_KDIM), F32),
                        pltpu.VMEM((nc, A_VDIM, A_KDIM), BF16),
                        pltpu.VMEM((nc, A_VDIM, L), BF16),
                        pltpu.VMEM((2, L, 2 * L), BF16), pltpu.VMEM((2, L, L), F32)],
        compiler_params=_params("parallel", "parallel"),
        name="gla",
    )(proj, proj, proj, proj, proj, lf, lf, lf, lf, out_gain.reshape(A_HEADS, 1, A_VDIM))


def _dft_tables(t_len):
    n = B_GDIM
    kk = (np.arange(n)[:, None] * np.arange(n)[None, :]) % n
    ang = 2.0 * np.pi * kk / n
    lane = np.concatenate([np.cos(ang), np.sin(ang)], axis=1).astype(np.float32)
    half = t_len // 2
    t_out = np.arange(half)[:, None]
    nn = np.arange(half)[None, :]
    tabs = []
    for t_in in (2 * nn, 2 * nn + 1):
        a = 2.0 * np.pi * ((t_in * t_out) % t_len) / t_len
        tabs.append(np.concatenate([np.cos(a), -np.sin(a)], axis=1).astype(np.float32))
    return lane, tabs[0], tabs[1]


def _fnet_kernel(u_ref, lane_ref, even_ref, odd_ref, o_ref, y_ref):
    t_len = u_ref.shape[1]
    half = t_len // 2
    lane = lane_ref[...]
    for g in range(B_GROUPS):
        ug = u_ref[0, :, g * B_GDIM:(g + 1) * B_GDIM].astype(BF16)
        y = jnp.dot(ug, lane, preferred_element_type=F32)
        y_ref[2 * g] = y[:, :B_GDIM]
        y_ref[2 * g + 1] = y[:, B_GDIM:]

    def rows_of_parity(parity):
        sel = pl.ds(parity, half, stride=2)
        parts = [jnp.concatenate([y_ref[2 * g + cs, sel, :] for g in range(B_GROUPS)], axis=1) for cs in (0, 1)]
        return jnp.concatenate(parts, axis=0).astype(BF16)

    norm = 1.0 / np.sqrt(float(t_len) * B_GDIM)
    e = jnp.dot(even_ref[...], rows_of_parity(0), preferred_element_type=F32)
    o = jnp.dot(odd_ref[...], rows_of_parity(1), preferred_element_type=F32)
    o_ref[0, :half, :] = ((e + o) * norm).astype(o_ref.dtype)
    o_ref[0, half:, :] = ((e - o) * norm).astype(o_ref.dtype)


def _fnet(proj):
    b, t, n_all = proj.shape
    half = t // 2
    lane, even, odd = (jnp.asarray(a).astype(BF16) for a in _dft_tables(t))
    return pl.pallas_call(
        _fnet_kernel,
        grid=(b,),
        in_specs=[
            pl.BlockSpec((1, t, B_WIDTH), lambda bi: (bi, 0, n_all // B_WIDTH - 1)),
            _resident((B_GDIM, 2 * B_GDIM), lambda bi: (0, 0)),
            _resident((half, t), lambda bi: (0, 0)),
            _resident((half, t), lambda bi: (0, 0)),
        ],
        out_specs=pl.BlockSpec((1, t, B_WIDTH), lambda bi: (bi, 0, 0)),
        out_shape=jax.ShapeDtypeStruct((b, t, B_WIDTH), BF16),
        scratch_shapes=[pltpu.VMEM((2 * B_GROUPS, t, B_GDIM), F32)],
        compiler_params=_params("arbitrary"),
        name="fnet",
    )(proj, lane, even, odd)


ATTN_ITEMS_AFTER_LAST_OUTPUT = 3
ATTN_SPAN = QBLOCK + 2 * WINDOW
ATTN_GROUP_HEADS = [[kv * C_GROUP + g for g in range(C_GROUP)] for kv in range(C_KV)]


def _attn_write_bias_tables(bias_ref):
    c = lax.broadcasted_iota(jnp.int32, (ATTN_SPAN, QBLOCK), 0)
    r = lax.broadcasted_iota(jnp.int32, (ATTN_SPAN, QBLOCK), 1)
    dist = jnp.abs(r + WINDOW - c)
    dist_f = dist.astype(F32)
    for variant, (c_lo, c_hi) in enumerate(((WINDOW, ATTN_SPAN), (0, ATTN_SPAN), (0, WINDOW + QBLOCK))):
        valid = (dist <= WINDOW) & (c >= c_lo) & (c < c_hi)
        bias_ref[variant] = jnp.where(valid, -LOG2_E * dist_f, NEG_BIG)


def _attn_block_stages(q_blk, k_span, v_span, variant, sink_ref, bias_ref, store):
    scale = C_HDIM ** -0.5 * LOG2_E
    tn_dims = (((0,), (0,)), ((), ()))
    held = {}

    def logits():
        held["s"] = []
        for kv in range(C_KV):
            q = jnp.concatenate([q_blk(h) for h in ATTN_GROUP_HEADS[kv]], axis=0)
            q = (q.astype(F32) * scale).astype(BF16)
            held["s"].append(lax.dot_general(k_span(kv), q, NT_DIMS, preferred_element_type=F32))

    def softmax():
        held["p"], held["denom"] = [], []
        for kv in range(C_KV):
            base = bias_ref[variant]
            slopes = [2.0 ** (-8.0 * (h + 1) / C_HEADS) for h in ATTN_GROUP_HEADS[kv]]
            s = held["s"][kv] + jnp.concatenate([slope * base for slope in slopes], axis=1)
            sink = jnp.concatenate([jnp.full((1, QBLOCK), sink_ref[h] * LOG2_E, F32)
                                    for h in ATTN_GROUP_HEADS[kv]], axis=1)
            m = jnp.maximum(jnp.max(s, axis=0, keepdims=True), sink)
            p = jnp.exp2(s - m)
            held["denom"].append(jnp.sum(p, axis=0, keepdims=True) + jnp.exp2(sink - m))
            held["p"].append(p.astype(BF16))

    def output():
        outs_t = [lax.dot_general(v_span(kv), held["p"][kv], tn_dims, preferred_element_type=F32)
                  for kv in range(C_KV)]
        outs = []
        for kv in range(C_KV):
            o_t = outs_t[kv] / held["denom"][kv]
            outs += [o_t[:, g * QBLOCK:(g + 1) * QBLOCK].T for g in range(C_GROUP)]
        store(jnp.concatenate(outs, axis=1))

    return logits, softmax, output


def _attn_ffn_kernel(sink_ref, q_ref, kp_ref, km_ref, kn_ref, vp_ref, vm_ref, vn_ref,
                     wo_ref, x_ref, gains_ref, w1_hbm, w3_hbm, w2_hbm, o_ref,
                     bias_ref, att_ref, act_ref, wob_ref, w1b, w3b, w2b, stage_up, stage_down, sem_up, sem_down,
                     *, blocks_per_seq, layer):
    s = pl.program_id(0)
    tm = q_ref.shape[0]
    nq = tm // QBLOCK
    slot_attn = s % 2
    slot_ffn = (s + 1) % 2

    @pl.when(s == 0)
    def _():
        _attn_write_bias_tables(bias_ref)

    _cast_weight_once(wo_ref, wob_ref)
    _load_ffn_weights_once(layer, (w1_hbm, w3_hbm, w2_hbm), (w1b, w3b, w2b), stage_up, stage_down, sem_up, sem_down)

    seq_pos = jnp.minimum(s, pl.num_programs(0) - 2) % blocks_per_seq
    first = jnp.where(seq_pos == 0, 0, 1)
    last = jnp.where(seq_pos == blocks_per_seq - 1, 2, 1)

    def key_piece(prev_ref, main_ref, next_ref, i, cols):
        if i == 0:
            return prev_ref[:, cols]
        if i == nq + 1:
            return next_ref[:, cols]
        return main_ref[(i - 1) * QBLOCK:i * QBLOCK, cols]

    def span_of(prev_ref, main_ref, next_ref, jq):
        def get(kv):
            cols = slice(kv * C_HDIM, (kv + 1) * C_HDIM)
            return jnp.concatenate([key_piece(prev_ref, main_ref, next_ref, jq + i, cols) for i in range(3)], axis=0)
        return get

    def attention_stages():
        stages = []
        for jq in range(nq):
            rows = slice(jq * QBLOCK, (jq + 1) * QBLOCK)
            variant = first if jq == 0 else (last if jq == nq - 1 else 1)

            def store(o, rows=rows):
                att_ref[slot_attn, rows, :] = o.astype(att_ref.dtype)

            stages.append(_attn_block_stages(
                lambda h, rows=rows: q_ref[rows, h * C_HDIM:(h + 1) * C_HDIM],
                span_of(kp_ref, km_ref, kn_ref, jq), span_of(vp_ref, vm_ref, vn_ref, jq),
                variant, sink_ref, bias_ref, store))
        return stages

    def mixer_items():
        return _mix_ffn_items(lambda p, rs: att_ref[slot_ffn, rs, :], [wob_ref],
                              x_ref, gains_ref, w1b, w3b, w2b, o_ref, act_ref)

    last_step = pl.num_programs(0) - 1

    @pl.when(s == 0)
    def _():
        for logits, softmax, output in attention_stages():
            logits()
            softmax()
            output()

    @pl.when(s == last_step)
    def _():
        for item in mixer_items():
            item()

    @pl.when((s > 0) & (s < last_step))
    def _():
        stages, items = attention_stages(), mixer_items()
        items.pop(0)()
        share = -(-len(items) // nq)
        pending_output = None
        for jq, (logits, softmax, output) in enumerate(stages):
            logits()
            if pending_output is not None:
                pending_output()
            softmax()
            mine = items[jq * share:(jq + 1) * share]
            after = ATTN_ITEMS_AFTER_LAST_OUTPUT if jq == nq - 1 else 0
            for item in mine[:len(mine) - after]:
                item()
            if jq == nq - 1:
                output()
            else:
                pending_output = output
            for item in mine[len(mine) - after:]:
                item()


def _attn_ffn(qkv, sink, t, wo_all, wo_layer, x, gains, w1_all, w3_all, w2_all, layer):
    m, d = x.shape
    dff = w1_all.shape[2]
    tm = _row_tile(m)
    assert WINDOW == QBLOCK and tm % QBLOCK == 0 and t % tm == 0 and t // QBLOCK >= 2 and dff % COL_TILE == 0
    qw = C_HEADS * C_HDIM
    kw = C_KV * C_HDIM
    assert qw % kw == 0 and wo_all.shape[1] == qw
    k_col, v_col = qw // kw, qw // kw + 1
    n = m // tm
    per = tm // QBLOCK
    n_qb = m // QBLOCK

    def attn_block(s):
        return jnp.minimum(s, n - 1)

    def main_spec(width, col):
        return pl.BlockSpec((tm, width), lambda s: (attn_block(s), col))

    def edge_spec(col, after):
        def index(s):
            qb = attn_block(s) * per + (per if after else -1)
            return (jnp.clip(qb, 0, n_qb - 1), col)
        return pl.BlockSpec((QBLOCK, kw), index)

    def ffn_rows(s):
        return (jnp.maximum(s - 1, 0), 0)

    return pl.pallas_call(
        functools.partial(_attn_ffn_kernel, blocks_per_seq=t // tm, layer=layer),
        grid=(n + 1,),
        in_specs=[pl.BlockSpec(memory_space=pltpu.SMEM),
                  main_spec(qw, 0),
                  edge_spec(k_col, False), main_spec(kw, k_col), edge_spec(k_col, True),
                  edge_spec(v_col, False), main_spec(kw, v_col), edge_spec(v_col, True),
                  _resident((None, qw, d), lambda s: (wo_layer, 0, 0)),
                  pl.BlockSpec((tm, d), ffn_rows),
                  pl.BlockSpec(gains.shape, lambda s: (0, 0)),
                  pl.BlockSpec(memory_space=pl.ANY), pl.BlockSpec(memory_space=pl.ANY),
                  pl.BlockSpec(memory_space=pl.ANY)],
        out_specs=pl.BlockSpec((tm, d), ffn_rows),
        out_shape=jax.ShapeDtypeStruct((m, d), F32),
        scratch_shapes=[pltpu.VMEM((3, ATTN_SPAN, QBLOCK), F32),
                        pltpu.VMEM((2, tm, qw), BF16),
                        pltpu.VMEM((tm, dff + d), BF16),
                        pltpu.VMEM((qw, d), BF16)] + _ffn_weight_scratch(d, dff),
        compiler_params=_params("arbitrary"),
        name="attn_ffn",
    )(sink.astype(F32), qkv, qkv, qkv, qkv, qkv, qkv, qkv, wo_all, x, gains, w1_all, w3_all, w2_all)


def kernel(x, norm_gains, ab_w_in, ab_lb_table, ab_out_gain, ab_w_out, c_w_qkv, c_sink, c_w_out,
           ffn_w1, ffn_w3, ffn_w2):
    b, t, d = x.shape
    m = b * t
    depth = norm_gains.shape[0]
    xf = x.reshape(m, d)
    for layer in range(depth):
        gains = norm_gains[layer]
        if layer % 2 == 0:
            e = layer // 2
            proj, lf = _in_proj(xf, gains[0], ab_w_in, e, ab_lb_table)
            proj = proj.reshape(b, t, AB_IN)
            o = _gla(proj, lf.reshape(b, t, 4 * A_FDIM), ab_out_gain[e]).reshape(m, A_WIDTH)
            fo = _fnet(proj).reshape(m, B_WIDTH)
            xf = _mix_ffn([o, fo], ab_w_out, e, xf, gains, ffn_w1, ffn_w3, ffn_w2, layer)
        else:
            w_idx = layer // 2
            qkv = _norm_matmul(xf, gains[0], c_w_qkv, w_idx, BF16)
            xf = _attn_ffn(qkv, c_sink[w_idx], t, c_w_out, w_idx, xf, gains, ffn_w1, ffn_w3, ffn_w2, layer)
    return xf.reshape(b, t, d)
```

```python
import functools

import numpy as np
import jax
import jax.numpy as jnp
from jax import lax
from jax.experimental import pallas as pl
from jax.experimental.pallas import tpu as pltpu

F32 = jnp.float32
BF16 = jnp.bfloat16

EPS = 1e-6
A_HEADS = 4
A_KDIM = 128
A_VDIM = 128
A_FDIM = A_HEADS * A_KDIM
A_WIDTH = A_HEADS * A_VDIM
SUB = 32
SUB_SHIFT = 5
NSUB = 4
GLA_CHUNK = SUB * NSUB
CHUNKS_PER_STEP = 16
EMIT_CHUNKS_PER_STEP = 16
B_GROUPS = 4
B_GDIM = 128
B_WIDTH = B_GROUPS * B_GDIM
AB_IN = 3 * A_FDIM + 2 * A_WIDTH + B_WIDTH
C_HEADS = 16
C_KV = 4
C_GROUP = C_HEADS // C_KV
C_HDIM = 64
WINDOW = 128
QBLOCK = 128
LOG2_E = 1.4426950408889634
NEG_BIG = -1e30

V7X_VMEM_LIMIT_BYTES = 58 * 1024 * 1024
ROW_TILE = 512
SUB_ROWS = 256
COL_TILE = 256

NT_DIMS = (((1,), (1,)), ((), ()))


def _params(*semantics):
    return pltpu.CompilerParams(dimension_semantics=semantics, vmem_limit_bytes=V7X_VMEM_LIMIT_BYTES)


def _rms_scale(y):
    return lax.rsqrt(jnp.mean(y * y, axis=-1, keepdims=True) + EPS)


def _resident(block_shape, index_map):
    return pl.BlockSpec(block_shape, index_map, pipeline_mode=pl.Buffered(1))


def _row_tile(m, want=ROW_TILE):
    return want if m % want == 0 else (ROW_TILE if m % ROW_TILE == 0 else GLA_CHUNK)


def _cast_weight_once(w_ref, wb_ref):
    @pl.when(pl.program_id(0) == 0)
    def _():
        rows = w_ref.shape[0]
        step = min(rows, 256)
        for r0 in range(0, rows, step):
            wb_ref[r0:r0 + step, :] = w_ref[r0:r0 + step, :].astype(BF16)


def _norm_matmul_kernel(x_ref, g_ref, w_ref, o_ref, wb_ref):
    _cast_weight_once(w_ref, wb_ref)
    tm = x_ref.shape[0]
    n = w_ref.shape[1]
    sub = min(SUB_ROWS, tm)
    tn = 512 if n % 512 == 0 else COL_TILE
    hns = []
    for r in range(tm // sub):
        x = x_ref[r * sub:(r + 1) * sub, :]
        hns.append((x * _rms_scale(x) * g_ref[...]).astype(BF16))
    for r, hn in enumerate(hns):
        rs = slice(r * sub, (r + 1) * sub)
        for c in range(n // tn):
            cs = slice(c * tn, (c + 1) * tn)
            o_ref[rs, cs] = jnp.dot(hn, wb_ref[:, cs], preferred_element_type=F32).astype(o_ref.dtype)


def _norm_matmul(x, gain, w_all, layer, out_dtype):
    m, d = x.shape
    n = w_all.shape[2]
    tm = _row_tile(m, 2 * ROW_TILE)
    return pl.pallas_call(
        _norm_matmul_kernel,
        grid=(m // tm,),
        in_specs=[
            pl.BlockSpec((tm, d), lambda i: (i, 0)),
            pl.BlockSpec((1, d), lambda i: (0, 0)),
            _resident((None, d, n), lambda i: (layer, 0, 0)),
        ],
        out_specs=pl.BlockSpec((tm, n), lambda i: (i, 0)),
        out_shape=jax.ShapeDtypeStruct((m, n), out_dtype),
        scratch_shapes=[pltpu.VMEM((d, n), BF16)],
        compiler_params=_params("arbitrary"),
        name="norm_matmul",
    )(x, gain.reshape(1, d), w_all)


def _in_proj_kernel(x_ref, g_ref, w_ref, lbt_ref, o_ref, lf_ref, wb_ref, *, layer_e):
    _cast_weight_once(w_ref, wb_ref)
    tm = x_ref.shape[0]
    sub = min(SUB_ROWS, tm)
    tn = A_FDIM

    tab = lbt_ref[...]
    ex = jnp.exp(tab - jnp.max(tab, axis=0, keepdims=True))
    sm = ex / jnp.sum(ex, axis=0, keepdims=True)
    lb = sm[0]
    for i in range(1, layer_e + 1):
        lb = lb + sm[i]

    hns = []
    for r in range(tm // sub):
        x = x_ref[r * sub:(r + 1) * sub, :]
        hns.append((x * _rms_scale(x) * g_ref[...]).astype(BF16))
    gate_cols = (1, 2)
    z = {(r, c): jnp.dot(hn, wb_ref[:, c * tn:(c + 1) * tn], preferred_element_type=F32)
         for r, hn in enumerate(hns) for c in gate_cols}
    for (r, c), y in z.items():
        rs = slice(r * sub, (r + 1) * sub)
        lb_d = lb[c - 1:c]
        f = lb_d + (1.0 - lb_d) * jax.nn.sigmoid(y)
        lf = jnp.log(f)
        hi = lf.astype(BF16)
        lf_ref[rs, (c - 1) * tn:c * tn] = hi
        lf_ref[rs, (c + 1) * tn:(c + 2) * tn] = (lf - hi.astype(F32)).astype(BF16)
        o_ref[rs, c * tn:(c + 1) * tn] = (1.0 - f).astype(o_ref.dtype)
    for r, hn in enumerate(hns):
        rs = slice(r * sub, (r + 1) * sub)
        for c in range(AB_IN // tn):
            if c not in gate_cols:
                cs = slice(c * tn, (c + 1) * tn)
                y = jnp.dot(hn, wb_ref[:, cs], preferred_element_type=F32)
                if c == 0:
                    y = y * (A_KDIM ** -0.5)
                elif c == 4:
                    y = y * jax.nn.sigmoid(y)
                o_ref[rs, cs] = y.astype(o_ref.dtype)


def _in_proj(x, gain, w_all, layer_e, lb_table):
    m, d = x.shape
    assert w_all.shape[2] == AB_IN and AB_IN % A_FDIM == 0
    tm = _row_tile(m, 2 * ROW_TILE)
    return pl.pallas_call(
        functools.partial(_in_proj_kernel, layer_e=layer_e),
        grid=(m // tm,),
        in_specs=[
            pl.BlockSpec((tm, d), lambda i: (i, 0)),
            pl.BlockSpec((1, d), lambda i: (0, 0)),
            _resident((None, d, AB_IN), lambda i: (layer_e, 0, 0)),
            pl.BlockSpec(lb_table.shape, lambda i: (0, 0, 0)),
        ],
        out_specs=[pl.BlockSpec((tm, AB_IN), lambda i: (i, 0)), pl.BlockSpec((tm, 4 * A_FDIM), lambda i: (i, 0))],
        out_shape=[jax.ShapeDtypeStruct((m, AB_IN), BF16), jax.ShapeDtypeStruct((m, 4 * A_FDIM), BF16)],
        scratch_shapes=[pltpu.VMEM((d, AB_IN), BF16)],
        compiler_params=_params("arbitrary"),
        name="in_proj",
    )(x, gain.reshape(1, d), w_all, lb_table)


def _mix_ffn_items(load_mixed, wo_refs, x_ref, gains_ref, w1_ref, w3_ref, w2_ref, o_ref, act_ref):
    tm = x_ref.shape[0]
    dff = w1_ref.shape[1]
    sub = min(SUB_ROWS, tm)
    tf = COL_TILE
    g_mix, g_in, g_out = gains_ref[1:2, :], gains_ref[2:3, :], gains_ref[3:4, :]
    per_sub = []
    for r in range(tm // sub):
        rs = slice(r * sub, (r + 1) * sub)

        def head(rs=rs):
            y = None
            for p, wo_ref in enumerate(wo_refs):
                part = jnp.dot(load_mixed(p, rs), wo_ref[...], preferred_element_type=F32)
                y = part if y is None else y + part
            x1 = x_ref[rs, :] + y * _rms_scale(y) * g_mix
            o_ref[rs, :] = x1
            act_ref[rs, dff:] = (x1 * _rms_scale(x1) * g_in).astype(BF16)

        def tile(f, rs=rs):
            cs = slice(f * tf, (f + 1) * tf)
            hn = act_ref[rs, dff:]
            h1 = jnp.dot(hn, w1_ref[:, cs], preferred_element_type=F32)
            h3 = jnp.dot(hn, w3_ref[:, cs], preferred_element_type=F32)
            act_ref[rs, cs] = (h1 * jax.nn.sigmoid(h1) * h3).astype(BF16)

        def tail(rs=rs):
            y2 = jnp.dot(act_ref[rs, :dff], w2_ref[...], preferred_element_type=F32)
            o_ref[rs, :] = o_ref[rs, :] + y2 * _rms_scale(y2) * g_out

        per_sub.append((head, [functools.partial(tile, f) for f in range(dff // tf)], tail))
    items = [per_sub[0][0]]
    for r, (_, tiles, tail) in enumerate(per_sub):
        items += tiles[:-2]
        if r + 1 < len(per_sub):
            items.append(per_sub[r + 1][0])
        items += tiles[-2:] + [tail]
    return items


FFN_STAGE_UP = (8, 32)
FFN_STAGE_DOWN = (4, 176)


def _ffn_weight_scratch(d, dff):
    (up_slots, up_rows), (down_slots, down_rows) = FFN_STAGE_UP, FFN_STAGE_DOWN
    assert d % up_rows == 0 and dff % down_rows == 0 and up_rows % 16 == 0 and down_rows % 16 == 0
    return [pltpu.VMEM((d, dff), BF16), pltpu.VMEM((d, dff), BF16), pltpu.VMEM((dff, d), BF16),
            pltpu.VMEM((up_slots, up_rows, dff), F32), pltpu.VMEM((down_slots, down_rows, d), F32),
            pltpu.SemaphoreType.DMA((up_slots,)), pltpu.SemaphoreType.DMA((down_slots,))]


def _load_ffn_weights_once(layer, w_hbm_refs, wb_refs, stage_up, stage_down, sem_up, sem_down):
    @pl.when(pl.program_id(0) == 0)
    def _():
        jobs, used = [], {}
        for w_hbm, wb, stage, sem in zip(w_hbm_refs, wb_refs, (stage_up, stage_up, stage_down),
                                         (sem_up, sem_up, sem_down)):
            slots, rows = stage.shape[0], stage.shape[1]
            for r0 in range(0, wb.shape[0], rows):
                n_before = used.get(id(stage), 0)
                used[id(stage)] = n_before + 1
                slot = n_before % slots
                prev = [j for j, job in enumerate(jobs) if job[2] is stage and job[5] == slot]
                jobs.append((w_hbm, wb, stage, sem, r0, slot, rows, prev[-1] if prev else -1))

        def copy(j):
            w_hbm, _, stage, sem, r0, slot, rows, _ = jobs[j]
            return pltpu.make_async_copy(w_hbm.at[layer, pl.ds(r0, rows), :], stage.at[slot], sem.at[slot])

        started = 0
        for j, (_, wb, stage, _, r0, slot, rows, _) in enumerate(jobs):
            while started < len(jobs) and jobs[started][7] < j:
                copy(started).start()
                started += 1
            copy(j).wait()
            wb[r0:r0 + rows, :] = stage[slot].astype(BF16)


def _mix_ffn_kernel(*refs, n_in, layer):
    a_refs, wo_refs = refs[:n_in], refs[n_in:2 * n_in]
    x_ref, gains_ref, w1_hbm, w3_hbm, w2_hbm, o_ref, act_ref = refs[2 * n_in:2 * n_in + 7]
    wob_refs = refs[2 * n_in + 7:3 * n_in + 7]
    w1b, w3b, w2b, stage_up, stage_down, sem_up, sem_down = refs[3 * n_in + 7:]
    for wo_ref, wob_ref in zip(wo_refs, wob_refs):
        _cast_weight_once(wo_ref, wob_ref)
    _load_ffn_weights_once(layer, (w1_hbm, w3_hbm, w2_hbm), (w1b, w3b, w2b), stage_up, stage_down, sem_up, sem_down)
    for item in _mix_ffn_items(lambda p, rs: a_refs[p][rs, :], wob_refs,
                               x_ref, gains_ref, w1b, w3b, w2b, o_ref, act_ref):
        item()


def _mix_ffn(a_list, wo_all, wo_layer, x, gains, w1_all, w3_all, w2_all, layer):
    m, d = x.shape
    n_in = len(a_list)
    dff = w1_all.shape[2]
    tm = _row_tile(m, 2 * ROW_TILE)
    k_piece = a_list[0].shape[1]
    assert all(a.shape[1] == k_piece for a in a_list) and k_piece * n_in == wo_all.shape[1]
    assert dff % COL_TILE == 0
    in_specs = [pl.BlockSpec((tm, k_piece), lambda i: (i, 0)) for _ in a_list]
    in_specs += [_resident((None, k_piece, d), lambda i, p=p: (wo_layer, p, 0)) for p in range(n_in)]
    in_specs += [
        pl.BlockSpec((tm, d), lambda i: (i, 0)),
        pl.BlockSpec(gains.shape, lambda i: (0, 0)),
        pl.BlockSpec(memory_space=pl.ANY), pl.BlockSpec(memory_space=pl.ANY), pl.BlockSpec(memory_space=pl.ANY),
    ]
    return pl.pallas_call(
        functools.partial(_mix_ffn_kernel, n_in=n_in, layer=layer),
        grid=(m // tm,),
        in_specs=in_specs,
        out_specs=pl.BlockSpec((tm, d), lambda i: (i, 0)),
        out_shape=jax.ShapeDtypeStruct((m, d), F32),
        scratch_shapes=([pltpu.VMEM((tm, dff + d), BF16)] + [pltpu.VMEM((k_piece, d), BF16)] * n_in
                        + _ffn_weight_scratch(d, dff)),
        compiler_params=_params("arbitrary"),
        name="mix_ffn",
    )(*a_list, *([wo_all] * n_in), x, gains, w1_all, w3_all, w2_all)


def _sub_blocks(a):
    return [a[i * SUB:(i + 1) * SUB] for i in range(NSUB)]


def _gla_running_log(lf_hi, lf_lo, tri2):
    return jnp.dot(tri2, jnp.concatenate([lf_hi, lf_lo], axis=0), preferred_element_type=F32)


def _gla_scores(q, k, g, v_t, rev):
    L, K = q.shape
    order = list(range(NSUB))[::-1] if rev else list(range(NSUB))
    last, mid = (0, SUB // 2) if rev else (SUB - 1, SUB // 2 - 1)
    g_p = [_sub_blocks(g)[i] for i in order]
    q_p = [_sub_blocks(q)[i] for i in order]
    k_p = [_sub_blocks(k)[i] for i in order]
    tot = [gp[last:last + 1] for gp in g_p]
    ref = [gp[mid:mid + 1] for gp in g_p]

    qr = [q_p[p] * jnp.exp(g_p[p] - ref[p]) for p in range(NSUB)]
    kr = [k_p[p] * jnp.exp(ref[p] - g_p[p]) for p in range(NSUB)]
    qs = [q_p[p] * jnp.exp(g_p[p]) for p in range(NSUB)]
    ko = [k_p[p] * jnp.exp(tot[p] - g_p[p]) for p in range(NSUB)]
    e1, e2, e3 = jnp.exp(tot[1]), jnp.exp(tot[2]), jnp.exp(tot[3])
    e23 = e2 * e3
    q_in = [qs[0], qs[1] * jnp.exp(tot[0]), qs[2] * jnp.exp(tot[0] + tot[1]),
            qs[3] * jnp.exp(tot[0] + tot[1] + tot[2])]
    k_out = [ko[0] * (e1 * e23), ko[1] * e23, ko[2] * e3, ko[3]]
    d_all = jnp.exp(tot[0] + tot[1] + tot[2] + tot[3])

    zero = jnp.zeros((SUB, K), F32)

    def rows(pieces):
        by_index = [None] * NSUB
        for p, i in enumerate(order):
            by_index[i] = pieces[p]
        return jnp.concatenate(by_index, axis=0).astype(BF16)

    lhs_off = jnp.concatenate([rows([zero, qs[1], zero, zero]),
                               rows([zero, zero, zero, qs[3]]),
                               rows([zero, zero, qs[2], qs[3] * e2])], axis=1)
    rhs_off = jnp.concatenate([rows([ko[0], zero, zero, zero]),
                               rows([zero, zero, ko[2], zero]),
                               rows([ko[0] * e1, ko[1], zero, zero])], axis=1)
    x_diag = lax.dot_general(rows(qr), rows(kr), NT_DIMS, preferred_element_type=F32)
    x_off = lax.dot_general(lhs_off, rhs_off, NT_DIMS, preferred_element_type=F32)
    kv = jnp.dot(v_t, rows(k_out), preferred_element_type=F32)
    return x_diag, x_off, rows(q_in), kv, d_all


def _gla_kernel(q_ref, kf_ref, kb_ref, v_ref, gate_ref, hif_ref, hib_ref, lof_ref, lob_ref, og_ref, o_ref,
                lhs_ref, kv_ref, dall_ref, stb_ref, vt_ref, tri_ref, diag_ref):
    t_len = q_ref.shape[1]
    L = GLA_CHUNK
    nc = t_len // L

    r = lax.broadcasted_iota(jnp.int32, (L, L), 0)
    c = lax.broadcasted_iota(jnp.int32, (L, L), 1)
    same = (r >> SUB_SHIFT) == (c >> SUB_SHIFT)
    for d, keep in enumerate((same & (c <= r), same & (c >= r))):
        ones = jnp.where(keep, 1.0, 0.0)
        diag_ref[d] = ones
        tri_ref[d] = jnp.concatenate([ones, ones], axis=1).astype(BF16)

    def chunk_rows(n):
        return pl.ds(pl.multiple_of(n * L, L), L)

    local_chunks = min(CHUNKS_PER_STEP, nc)
    emit_chunks = min(EMIT_CHUNKS_PER_STEP, nc)
    assert nc % local_chunks == 0 and nc % emit_chunks == 0

    def local_part(i, carry):
        chunks = [i * local_chunks + u for u in range(local_chunks)]
        jobs = [(u, d) for u in range(local_chunks) for d in (0, 1)]
        k_refs, hi_refs, lo_refs = (kf_ref, kb_ref), (hif_ref, hib_ref), (lof_ref, lob_ref)
        gates = []
        for u, d in jobs:
            rs = chunk_rows(chunks[u])
            g = _gla_running_log(hi_refs[d][0, rs, :], lo_refs[d][0, rs, :], tri_ref[d])
            gates.append((k_refs[d][0, rs, :].astype(F32), g))
        v_ts, qs = [], []
        for n in chunks:
            rs = chunk_rows(n)
            v_ts.append(v_ref[0, rs, :].astype(F32).T.astype(BF16))
            vt_ref[n] = v_ts[-1]
            qs.append(q_ref[0, rs, :].astype(F32))
        scores = [_gla_scores(qs[u], k, g, v_ts[u], rev=bool(d)) for (u, d), (k, g) in zip(jobs, gates)]
        for u, n in enumerate(chunks):
            a_sum, q_ins = None, []
            for (uu, d), (x_diag, x_off, q_in, kv, d_all) in zip(jobs, scores):
                if uu != u:
                    continue
                a = jnp.where(diag_ref[d] != 0.0, x_diag, x_off)
                a_sum = a if a_sum is None else a_sum + a
                q_ins.append(q_in)
                kv_ref[d, n] = kv
                dall_ref[d, n] = d_all
            lhs_ref[n] = jnp.concatenate([a_sum.astype(BF16)] + q_ins, axis=1)
        return carry

    lax.fori_loop(0, nc // local_chunks, local_part, 0)

    zero = jnp.zeros((A_VDIM, A_KDIM), F32)

    def backward_state(i, st):
        n = nc - 1 - i
        stb_ref[n] = st.astype(BF16)
        return st * dall_ref[1, n] + kv_ref[1, n]

    lax.fori_loop(0, nc, backward_state, zero)

    ones = jnp.ones((2 * A_VDIM, A_VDIM), BF16)

    def emit(i, st):
        chunks = [i * emit_chunks + u for u in range(emit_chunks)]
        outs = []
        for n in chunks:
            rhs_t = jnp.concatenate([vt_ref[n], st.astype(BF16), stb_ref[n]], axis=1)
            outs.append(lax.dot_general(lhs_ref[n], rhs_t, NT_DIMS, preferred_element_type=F32))
            st = st * dall_ref[0, n] + kv_ref[0, n]
        sqs = []
        for o in outs:
            o2 = o * o
            hi = o2.astype(BF16)
            lo = (o2 - hi.astype(F32)).astype(BF16)
            sqs.append(jnp.dot(jnp.concatenate([hi, lo], axis=1), ones, preferred_element_type=F32))
        for n, o, sq in zip(chunks, outs, sqs):
            rs = chunk_rows(n)
            o = o * lax.rsqrt(sq * (1.0 / A_VDIM) + EPS) * og_ref[0]
            o_ref[0, rs, :] = (o * gate_ref[0, rs, :].astype(F32)).astype(o_ref.dtype)
        return st

    lax.fori_loop(0, nc // emit_chunks, emit, zero)


def _gla(proj, lf, out_gain):
    b, t, _ = proj.shape
    kb = A_KDIM
    L = GLA_CHUNK
    assert A_KDIM == A_VDIM == L and t % (2 * L) == 0
    nc = t // L

    def col(group):
        return pl.BlockSpec((1, t, kb), lambda bi, h, group=group: (bi, 0, group * A_HEADS + h))

    return pl.pallas_call(
        _gla_kernel,
        grid=(b, A_HEADS),
        in_specs=[col(0), col(1), col(2), col(3), col(4), col(0), col(1), col(2), col(3),
                  pl.BlockSpec((1, 1, kb), lambda bi, h: (h, 0, 0))],
        out_specs=pl.BlockSpec((1, t, kb), lambda bi, h: (bi, 0, h)),
        out_shape=jax.ShapeDtypeStruct((b, t, A_WIDTH), BF16),
        scratch_shapes=[pltpu.VMEM((nc, L, 3 * L), BF16),
                        pltpu.VMEM((2, nc, A_VDIM, A_KDIM), F32),
                        pltpu.VMEM((2, nc, 1, A_KDIM), F32),
                        pltpu.VMEM((nc, A_VDIM, A_KDIM), BF16),
                        pltpu.VMEM((nc, A_VDIM, L), BF16),
                        pltpu.VMEM((2, L, 2 * L), BF16), pltpu.VMEM((2, L, L), F32)],
        compiler_params=_params("parallel", "parallel"),
        name="gla",
    )(proj, proj, proj, proj, proj, lf, lf, lf, lf, out_gain.reshape(A_HEADS, 1, A_VDIM))


def _dft_tables(t_len):
    n = B_GDIM
    kk = (np.arange(n)[:, None] * np.arange(n)[None, :]) % n
    ang = 2.0 * np.pi * kk / n
    lane = np.concatenate([np.cos(ang), np.sin(ang)], axis=1).astype(np.float32)
    half = t_len // 2
    t_out = np.arange(half)[:, None]
    nn = np.arange(half)[None, :]
    tabs = []
    for t_in in (2 * nn, 2 * nn + 1):
        a = 2.0 * np.pi * ((t_in * t_out) % t_len) / t_len
        tabs.append(np.concatenate([np.cos(a), -np.sin(a)], axis=1).astype(np.float32))
    return lane, tabs[0], tabs[1]


def _fnet_kernel(u_ref, lane_ref, even_ref, odd_ref, o_ref, y_ref):
    t_len = u_ref.shape[1]
    half = t_len // 2
    lane = lane_ref[...]
    for g in range(B_GROUPS):
        ug = u_ref[0, :, g * B_GDIM:(g + 1) * B_GDIM].astype(BF16)
        y = jnp.dot(ug, lane, preferred_element_type=F32)
        y_ref[2 * g] = y[:, :B_GDIM]
        y_ref[2 * g + 1] = y[:, B_GDIM:]

    def rows_of_parity(parity):
        sel = pl.ds(parity, half, stride=2)
        parts = [jnp.concatenate([y_ref[2 * g + cs, sel, :] for g in range(B_GROUPS)], axis=1) for cs in (0, 1)]
        return jnp.concatenate(parts, axis=0).astype(BF16)

    norm = 1.0 / np.sqrt(float(t_len) * B_GDIM)
    e = jnp.dot(even_ref[...], rows_of_parity(0), preferred_element_type=F32)
    o = jnp.dot(odd_ref[...], rows_of_parity(1), preferred_element_type=F32)
    o_ref[0, :half, :] = ((e + o) * norm).astype(o_ref.dtype)
    o_ref[0, half:, :] = ((e - o) * norm).astype(o_ref.dtype)


def _fnet(proj):
    b, t, n_all = proj.shape
    half = t // 2
    lane, even, odd = (jnp.asarray(a).astype(BF16) for a in _dft_tables(t))
    return pl.pallas_call(
        _fnet_kernel,
        grid=(b,),
        in_specs=[
            pl.BlockSpec((1, t, B_WIDTH), lambda bi: (bi, 0, n_all // B_WIDTH - 1)),
            _resident((B_GDIM, 2 * B_GDIM), lambda bi: (0, 0)),
            _resident((half, t), lambda bi: (0, 0)),
            _resident((half, t), lambda bi: (0, 0)),
        ],
        out_specs=pl.BlockSpec((1, t, B_WIDTH), lambda bi: (bi, 0, 0)),
        out_shape=jax.ShapeDtypeStruct((b, t, B_WIDTH), BF16),
        scratch_shapes=[pltpu.VMEM((2 * B_GROUPS, t, B_GDIM), F32)],
        compiler_params=_params("arbitrary"),
        name="fnet",
    )(proj, lane, even, odd)


ATTN_ITEMS_AFTER_LAST_OUTPUT = 3
ATTN_SPAN = QBLOCK + 2 * WINDOW
ATTN_GROUP_HEADS = [[kv * C_GROUP + g for g in range(C_GROUP)] for kv in range(C_KV)]


def _attn_write_bias_tables(bias_ref):
    c = lax.broadcasted_iota(jnp.int32, (ATTN_SPAN, QBLOCK), 0)
    r = lax.broadcasted_iota(jnp.int32, (ATTN_SPAN, QBLOCK), 1)
    dist = jnp.abs(r + WINDOW - c)
    dist_f = dist.astype(F32)
    for variant, (c_lo, c_hi) in enumerate(((WINDOW, ATTN_SPAN), (0, ATTN_SPAN), (0, WINDOW + QBLOCK))):
        valid = (dist <= WINDOW) & (c >= c_lo) & (c < c_hi)
        bias_ref[variant] = jnp.where(valid, -LOG2_E * dist_f, NEG_BIG)


def _attn_block_stages(q_blk, k_span, v_span, variant, sink_ref, bias_ref, store):
    scale = C_HDIM ** -0.5 * LOG2_E
    tn_dims = (((0,), (0,)), ((), ()))
    held = {}

    def logits():
        held["s"] = []
        for kv in range(C_KV):
            q = jnp.concatenate([q_blk(h) for h in ATTN_GROUP_HEADS[kv]], axis=0)
            q = (q.astype(F32) * scale).astype(BF16)
            held["s"].append(lax.dot_general(k_span(kv), q, NT_DIMS, preferred_element_type=F32))

    def softmax():
        held["p"], held["denom"] = [], []
        for kv in range(C_KV):
            base = bias_ref[variant]
            slopes = [2.0 ** (-8.0 * (h + 1) / C_HEADS) for h in ATTN_GROUP_HEADS[kv]]
            s = held["s"][kv] + jnp.concatenate([slope * base for slope in slopes], axis=1)
            sink = jnp.concatenate([jnp.full((1, QBLOCK), sink_ref[h] * LOG2_E, F32)
                                    for h in ATTN_GROUP_HEADS[kv]], axis=1)
            m = jnp.maximum(jnp.max(s, axis=0, keepdims=True), sink)
            p = jnp.exp2(s - m)
            held["denom"].append(jnp.sum(p, axis=0, keepdims=True) + jnp.exp2(sink - m))
            held["p"].append(p.astype(BF16))

    def output():
        outs_t = [lax.dot_general(v_span(kv), held["p"][kv], tn_dims, preferred_element_type=F32)
                  for kv in range(C_KV)]
        outs = []
        for kv in range(C_KV):
            o_t = outs_t[kv] / held["denom"][kv]
            outs += [o_t[:, g * QBLOCK:(g + 1) * QBLOCK].T for g in range(C_GROUP)]
        store(jnp.concatenate(outs, axis=1))

    return logits, softmax, output


def _attn_ffn_kernel(sink_ref, q_ref, kp_ref, km_ref, kn_ref, vp_ref, vm_ref, vn_ref,
                     wo_ref, x_ref, gains_ref, w1_hbm, w3_hbm, w2_hbm, o_ref,
                     bias_ref, att_ref, act_ref, wob_ref, w1b, w3b, w2b, stage_up, stage_down, sem_up, sem_down,
                     *, blocks_per_seq, layer):
    s = pl.program_id(0)
    tm = q_ref.shape[0]
    nq = tm // QBLOCK
    slot_attn = s % 2
    slot_ffn = (s + 1) % 2

    @pl.when(s == 0)
    def _():
        _attn_write_bias_tables(bias_ref)

    _cast_weight_once(wo_ref, wob_ref)
    _load_ffn_weights_once(layer, (w1_hbm, w3_hbm, w2_hbm), (w1b, w3b, w2b), stage_up, stage_down, sem_up, sem_down)

    seq_pos = jnp.minimum(s, pl.num_programs(0) - 2) % blocks_per_seq
    first = jnp.where(seq_pos == 0, 0, 1)
    last = jnp.where(seq_pos == blocks_per_seq - 1, 2, 1)

    def key_piece(prev_ref, main_ref, next_ref, i, cols):
        if i == 0:
            return prev_ref[:, cols]
        if i == nq + 1:
            return next_ref[:, cols]
        return main_ref[(i - 1) * QBLOCK:i * QBLOCK, cols]

    def span_of(prev_ref, main_ref, next_ref, jq):
        def get(kv):
            cols = slice(kv * C_HDIM, (kv + 1) * C_HDIM)
            return jnp.concatenate([key_piece(prev_ref, main_ref, next_ref, jq + i, cols) for i in range(3)], axis=0)
        return get

    def attention_stages():
        stages = []
        for jq in range(nq):
            rows = slice(jq * QBLOCK, (jq + 1) * QBLOCK)
            variant = first if jq == 0 else (last if jq == nq - 1 else 1)

            def store(o, rows=rows):
                att_ref[slot_attn, rows, :] = o.astype(att_ref.dtype)

            stages.append(_attn_block_stages(
                lambda h, rows=rows: q_ref[rows, h * C_HDIM:(h + 1) * C_HDIM],
                span_of(kp_ref, km_ref, kn_ref, jq), span_of(vp_ref, vm_ref, vn_ref, jq),
                variant, sink_ref, bias_ref, store))
        return stages

    def mixer_items():
        return _mix_ffn_items(lambda p, rs: att_ref[slot_ffn, rs, :], [wob_ref],
                              x_ref, gains_ref, w1b, w3b, w2b, o_ref, act_ref)

    last_step = pl.num_programs(0) - 1

    @pl.when(s == 0)
    def _():
        for logits, softmax, output in attention_stages():
            logits()
            softmax()
            output()

    @pl.when(s == last_step)
    def _():
        for item in mixer_items():
            item()

    @pl.when((s > 0) & (s < last_step))
    def _():
        stages, items = attention_stages(), mixer_items()
        items.pop(0)()
        share = -(-len(items) // nq)
        pending_output = None
        for jq, (logits, softmax, output) in enumerate(stages):
            logits()
            if pending_output is not None:
                pending_output()
            softmax()
            mine = items[jq * share:(jq + 1) * share]
            after = ATTN_ITEMS_AFTER_LAST_OUTPUT if jq == nq - 1 else 0
            for item in mine[:len(mine) - after]:
                item()
            if jq == nq - 1:
                output()
            else:
                pending_output = output
            for item in mine[len(mine) - after:]:
                item()


def _attn_ffn(qkv, sink, t, wo_all, wo_layer, x, gains, w1_all, w3_all, w2_all, layer):
    m, d = x.shape
    dff = w1_all.shape[2]
    tm = _row_tile(m)
    assert WINDOW == QBLOCK and tm % QBLOCK == 0 and t % tm == 0 and t // QBLOCK >= 2 and dff % COL_TILE == 0
    qw = C_HEADS * C_HDIM
    kw = C_KV * C_HDIM
    assert qw % kw == 0 and wo_all.shape[1] == qw
    k_col, v_col = qw // kw, qw // kw + 1
    n = m // tm
    per = tm // QBLOCK
    n_qb = m // QBLOCK

    def attn_block(s):
        return jnp.minimum(s, n - 1)

    def main_spec(width, col):
        return pl.BlockSpec((tm, width), lambda s: (attn_block(s), col))

    def edge_spec(col, after):
        def index(s):
            qb = attn_block(s) * per + (per if after else -1)
            return (jnp.clip(qb, 0, n_qb - 1), col)
        return pl.BlockSpec((QBLOCK, kw), index)

    def ffn_rows(s):
        return (jnp.maximum(s - 1, 0), 0)

    return pl.pallas_call(
        functools.partial(_attn_ffn_kernel, blocks_per_seq=t // tm, layer=layer),
        grid=(n + 1,),
        in_specs=[pl.BlockSpec(memory_space=pltpu.SMEM),
                  main_spec(qw, 0),
                  edge_spec(k_col, False), main_spec(kw, k_col), edge_spec(k_col, True),
                  edge_spec(v_col, False), main_spec(kw, v_col), edge_spec(v_col, True),
                  _resident((None, qw, d), lambda s: (wo_layer, 0, 0)),
                  pl.BlockSpec((tm, d), ffn_rows),
                  pl.BlockSpec(gains.shape, lambda s: (0, 0)),
                  pl.BlockSpec(memory_space=pl.ANY), pl.BlockSpec(memory_space=pl.ANY),
                  pl.BlockSpec(memory_space=pl.ANY)],
        out_specs=pl.BlockSpec((tm, d), ffn_rows),
        out_shape=jax.ShapeDtypeStruct((m, d), F32),
        scratch_shapes=[pltpu.VMEM((3, ATTN_SPAN, QBLOCK), F32),
                        pltpu.VMEM((2, tm, qw), BF16),
                        pltpu.VMEM((tm, dff + d), BF16),
                        pltpu.VMEM((qw, d), BF16)] + _ffn_weight_scratch(d, dff),
        compiler_params=_params("arbitrary"),
        name="attn_ffn",
    )(sink.astype(F32), qkv, qkv, qkv, qkv, qkv, qkv, qkv, wo_all, x, gains, w1_all, w3_all, w2_all)


def kernel(x, norm_gains, ab_w_in, ab_lb_table, ab_out_gain, ab_w_out, c_w_qkv, c_sink, c_w_out,
           ffn_w1, ffn_w3, ffn_w2):
    b, t, d = x.shape
    m = b * t
    depth = norm_gains.shape[0]
    xf = x.reshape(m, d)
    for layer in range(depth):
        gains = norm_gains[layer]
        if layer % 2 == 0:
            e = layer // 2
            proj, lf = _in_proj(xf, gains[0], ab_w_in, e, ab_lb_table)
            proj = proj.reshape(b, t, AB_IN)
            o = _gla(proj, lf.reshape(b, t, 4 * A_FDIM), ab_out_gain[e]).reshape(m, A_WIDTH)
            fo = _fnet(proj).reshape(m, B_WIDTH)
            xf = _mix_ffn([o, fo], ab_w_out, e, xf, gains, ffn_w1, ffn_w3, ffn_w2, layer)
        else:
            w_idx = layer // 2
            qkv = _norm_matmul(xf, gains[0], c_w_qkv, w_idx, BF16)
            xf = _attn_ffn(qkv, c_sink[w_idx], t, c_w_out, w_idx, xf, gains, ffn_w1, ffn_w3, ffn_w2, layer)
    return xf.reshape(b, t, d)
```

```python
import functools

import numpy as np
import jax
import jax.numpy as jnp
from jax import lax
from jax.experimental import pallas as pl
from jax.experimental.pallas import tpu as pltpu

F32 = jnp.float32
BF16 = jnp.bfloat16

EPS = 1e-6
A_HEADS = 4
A_KDIM = 128
A_VDIM = 128
A_FDIM = A_HEADS * A_KDIM
A_WIDTH = A_HEADS * A_VDIM
SUB = 32
SUB_SHIFT = 5
NSUB = 4
GLA_CHUNK = SUB * NSUB
CHUNKS_PER_STEP = 16
EMIT_CHUNKS_PER_STEP = 16
B_GROUPS = 4
B_GDIM = 128
B_WIDTH = B_GROUPS * B_GDIM
AB_IN = 3 * A_FDIM + 2 * A_WIDTH + B_WIDTH
C_HEADS = 16
C_KV = 4
C_GROUP = C_HEADS // C_KV
C_HDIM = 64
WINDOW = 128
QBLOCK = 128
LOG2_E = 1.4426950408889634
NEG_BIG = -1e30

V7X_VMEM_LIMIT_BYTES = 58 * 1024 * 1024
ROW_TILE = 512
SUB_ROWS = 256
COL_TILE = 256

NT_DIMS = (((1,), (1,)), ((), ()))


def _params(*semantics):
    return pltpu.CompilerParams(dimension_semantics=semantics, vmem_limit_bytes=V7X_VMEM_LIMIT_BYTES)


def _rms_scale(y):
    return lax.rsqrt(jnp.mean(y * y, axis=-1, keepdims=True) + EPS)


def _resident(block_shape, index_map):
    return pl.BlockSpec(block_shape, index_map, pipeline_mode=pl.Buffered(1))


def _row_tile(m, want=ROW_TILE):
    return want if m % want == 0 else (ROW_TILE if m % ROW_TILE == 0 else GLA_CHUNK)


def _cast_weight_once(w_ref, wb_ref):
    @pl.when(pl.program_id(0) == 0)
    def _():
        rows = w_ref.shape[0]
        step = min(rows, 256)
        for r0 in range(0, rows, step):
            wb_ref[r0:r0 + step, :] = w_ref[r0:r0 + step, :].astype(BF16)


def _norm_matmul_kernel(x_ref, g_ref, w_ref, o_ref, wb_ref):
    _cast_weight_once(w_ref, wb_ref)
    tm = x_ref.shape[0]
    n = w_ref.shape[1]
    sub = min(SUB_ROWS, tm)
    tn = 512 if n % 512 == 0 else COL_TILE
    hns = []
    for r in range(tm // sub):
        x = x_ref[r * sub:(r + 1) * sub, :]
        hns.append((x * _rms_scale(x) * g_ref[...]).astype(BF16))
    for r, hn in enumerate(hns):
        rs = slice(r * sub, (r + 1) * sub)
        for c in range(n // tn):
            cs = slice(c * tn, (c + 1) * tn)
            o_ref[rs, cs] = jnp.dot(hn, wb_ref[:, cs], preferred_element_type=F32).astype(o_ref.dtype)


def _norm_matmul(x, gain, w_all, layer, out_dtype):
    m, d = x.shape
    n = w_all.shape[2]
    tm = _row_tile(m, 2 * ROW_TILE)
    return pl.pallas_call(
        _norm_matmul_kernel,
        grid=(m // tm,),
        in_specs=[
            pl.BlockSpec((tm, d), lambda i: (i, 0)),
            pl.BlockSpec((1, d), lambda i: (0, 0)),
            _resident((None, d, n), lambda i: (layer, 0, 0)),
        ],
        out_specs=pl.BlockSpec((tm, n), lambda i: (i, 0)),
        out_shape=jax.ShapeDtypeStruct((m, n), out_dtype),
        scratch_shapes=[pltpu.VMEM((d, n), BF16)],
        compiler_params=_params("arbitrary"),
        name="norm_matmul",
    )(x, gain.reshape(1, d), w_all)


def _in_proj_kernel(x_ref, g_ref, w_ref, lbt_ref, o_ref, lf_ref, wb_ref, *, layer_e):
    _cast_weight_once(w_ref, wb_ref)
    tm = x_ref.shape[0]
    sub = min(SUB_ROWS, tm)
    tn = A_FDIM

    tab = lbt_ref[...]
    ex = jnp.exp(tab - jnp.max(tab, axis=0, keepdims=True))
    sm = ex / jnp.sum(ex, axis=0, keepdims=True)
    lb = sm[0]
    for i in range(1, layer_e + 1):
        lb = lb + sm[i]

    hns = []
    for r in range(tm // sub):
        x = x_ref[r * sub:(r + 1) * sub, :]
        hns.append((x * _rms_scale(x) * g_ref[...]).astype(BF16))
    gate_cols = (1, 2)
    z = {(r, c): jnp.dot(hn, wb_ref[:, c * tn:(c + 1) * tn], preferred_element_type=F32)
         for r, hn in enumerate(hns) for c in gate_cols}
    for (r, c), y in z.items():
        rs = slice(r * sub, (r + 1) * sub)
        lb_d = lb[c - 1:c]
        f = lb_d + (1.0 - lb_d) * jax.nn.sigmoid(y)
        lf = jnp.log(f)
        hi = lf.astype(BF16)
        lf_ref[rs, (c - 1) * tn:c * tn] = hi
        lf_ref[rs, (c + 1) * tn:(c + 2) * tn] = (lf - hi.astype(F32)).astype(BF16)
        o_ref[rs, c * tn:(c + 1) * tn] = (1.0 - f).astype(o_ref.dtype)
    for r, hn in enumerate(hns):
        rs = slice(r * sub, (r + 1) * sub)
        for c in range(AB_IN // tn):
            if c not in gate_cols:
                cs = slice(c * tn, (c + 1) * tn)
                y = jnp.dot(hn, wb_ref[:, cs], preferred_element_type=F32)
                if c == 0:
                    y = y * (A_KDIM ** -0.5)
                elif c == 4:
                    y = y * jax.nn.sigmoid(y)
                o_ref[rs, cs] = y.astype(o_ref.dtype)


def _in_proj(x, gain, w_all, layer_e, lb_table):
    m, d = x.shape
    assert w_all.shape[2] == AB_IN and AB_IN % A_FDIM == 0
    tm = _row_tile(m, 2 * ROW_TILE)
    return pl.pallas_call(
        functools.partial(_in_proj_kernel, layer_e=layer_e),
        grid=(m // tm,),
        in_specs=[
            pl.BlockSpec((tm, d), lambda i: (i, 0)),
            pl.BlockSpec((1, d), lambda i: (0, 0)),
            _resident((None, d, AB_IN), lambda i: (layer_e, 0, 0)),
            pl.BlockSpec(lb_table.shape, lambda i: (0, 0, 0)),
        ],
        out_specs=[pl.BlockSpec((tm, AB_IN), lambda i: (i, 0)), pl.BlockSpec((tm, 4 * A_FDIM), lambda i: (i, 0))],
        out_shape=[jax.ShapeDtypeStruct((m, AB_IN), BF16), jax.ShapeDtypeStruct((m, 4 * A_FDIM), BF16)],
        scratch_shapes=[pltpu.VMEM((d, AB_IN), BF16)],
        compiler_params=_params("arbitrary"),
        name="in_proj",
    )(x, gain.reshape(1, d), w_all, lb_table)


def _mix_ffn_items(load_mixed, wo_refs, x_ref, gains_ref, w1_ref, w3_ref, w2_ref, o_ref, act_ref):
    tm = x_ref.shape[0]
    dff = w1_ref.shape[1]
    sub = min(SUB_ROWS, tm)
    tf = COL_TILE
    g_mix, g_in, g_out = gains_ref[1:2, :], gains_ref[2:3, :], gains_ref[3:4, :]
    per_sub = []
    for r in range(tm // sub):
        rs = slice(r * sub, (r + 1) * sub)

        def head(rs=rs):
            y = None
            for p, wo_ref in enumerate(wo_refs):
                part = jnp.dot(load_mixed(p, rs), wo_ref[...], preferred_element_type=F32)
                y = part if y is None else y + part
            x1 = x_ref[rs, :] + y * _rms_scale(y) * g_mix
            o_ref[rs, :] = x1
            act_ref[rs, dff:] = (x1 * _rms_scale(x1) * g_in).astype(BF16)

        def tile(f, rs=rs):
            cs = slice(f * tf, (f + 1) * tf)
            hn = act_ref[rs, dff:]
            h1 = jnp.dot(hn, w1_ref[:, cs], preferred_element_type=F32)
            h3 = jnp.dot(hn, w3_ref[:, cs], preferred_element_type=F32)
            act_ref[rs, cs] = (h1 * jax.nn.sigmoid(h1) * h3).astype(BF16)

        def tail(rs=rs):
            y2 = jnp.dot(act_ref[rs, :dff], w2_ref[...], preferred_element_type=F32)
            o_ref[rs, :] = o_ref[rs, :] + y2 * _rms_scale(y2) * g_out

        per_sub.append((head, [functools.partial(tile, f) for f in range(dff // tf)], tail))
    items = [per_sub[0][0]]
    for r, (_, tiles, tail) in enumerate(per_sub):
        items += tiles[:-2]
        if r + 1 < len(per_sub):
            items.append(per_sub[r + 1][0])
        items += tiles[-2:] + [tail]
    return items


FFN_STAGE_UP = (8, 32)
FFN_STAGE_DOWN = (4, 176)


def _ffn_weight_scratch(d, dff):
    (up_slots, up_rows), (down_slots, down_rows) = FFN_STAGE_UP, FFN_STAGE_DOWN
    assert d % up_rows == 0 and dff % down_rows == 0 and up_rows % 16 == 0 and down_rows % 16 == 0
    return [pltpu.VMEM((d, dff), BF16), pltpu.VMEM((d, dff), BF16), pltpu.VMEM((dff, d), BF16),
            pltpu.VMEM((up_slots, up_rows, dff), F32), pltpu.VMEM((down_slots, down_rows, d), F32),
            pltpu.SemaphoreType.DMA((up_slots,)), pltpu.SemaphoreType.DMA((down_slots,))]


def _load_ffn_weights_once(layer, w_hbm_refs, wb_refs, stage_up, stage_down, sem_up, sem_down):
    @pl.when(pl.program_id(0) == 0)
    def _():
        jobs, used = [], {}
        for w_hbm, wb, stage, sem in zip(w_hbm_refs, wb_refs, (stage_up, stage_up, stage_down),
                                         (sem_up, sem_up, sem_down)):
            slots, rows = stage.shape[0], stage.shape[1]
            for r0 in range(0, wb.shape[0], rows):
                n_before = used.get(id(stage), 0)
                used[id(stage)] = n_before + 1
                slot = n_before % slots
                prev = [j for j, job in enumerate(jobs) if job[2] is stage and job[5] == slot]
                jobs.append((w_hbm, wb, stage, sem, r0, slot, rows, prev[-1] if prev else -1))

        def copy(j):
            w_hbm, _, stage, sem, r0, slot, rows, _ = jobs[j]
            return pltpu.make_async_copy(w_hbm.at[layer, pl.ds(r0, rows), :], stage.at[slot], sem.at[slot])

        started = 0
        for j, (_, wb, stage, _, r0, slot, rows, _) in enumerate(jobs):
            while started < len(jobs) and jobs[started][7] < j:
                copy(started).start(priority=started % 2)
                started += 1
            copy(j).wait()
            wb[r0:r0 + rows, :] = stage[slot].astype(BF16)


def _mix_ffn_kernel(*refs, n_in, layer):
    a_refs, wo_refs = refs[:n_in], refs[n_in:2 * n_in]
    x_ref, gains_ref, w1_hbm, w3_hbm, w2_hbm, o_ref, act_ref = refs[2 * n_in:2 * n_in + 7]
    wob_refs = refs[2 * n_in + 7:3 * n_in + 7]
    w1b, w3b, w2b, stage_up, stage_down, sem_up, sem_down = refs[3 * n_in + 7:]
    for wo_ref, wob_ref in zip(wo_refs, wob_refs):
        _cast_weight_once(wo_ref, wob_ref)
    _load_ffn_weights_once(layer, (w1_hbm, w3_hbm, w2_hbm), (w1b, w3b, w2b), stage_up, stage_down, sem_up, sem_down)
    for item in _mix_ffn_items(lambda p, rs: a_refs[p][rs, :], wob_refs,
                               x_ref, gains_ref, w1b, w3b, w2b, o_ref, act_ref):
        item()


def _mix_ffn(a_list, wo_all, wo_layer, x, gains, w1_all, w3_all, w2_all, layer):
    m, d = x.shape
    n_in = len(a_list)
    dff = w1_all.shape[2]
    tm = _row_tile(m, 2 * ROW_TILE)
    k_piece = a_list[0].shape[1]
    assert all(a.shape[1] == k_piece for a in a_list) and k_piece * n_in == wo_all.shape[1]
    assert dff % COL_TILE == 0
    in_specs = [pl.BlockSpec((tm, k_piece), lambda i: (i, 0)) for _ in a_list]
    in_specs += [_resident((None, k_piece, d), lambda i, p=p: (wo_layer, p, 0)) for p in range(n_in)]
    in_specs += [
        pl.BlockSpec((tm, d), lambda i: (i, 0)),
        pl.BlockSpec(gains.shape, lambda i: (0, 0)),
        pl.BlockSpec(memory_space=pl.ANY), pl.BlockSpec(memory_space=pl.ANY), pl.BlockSpec(memory_space=pl.ANY),
    ]
    return pl.pallas_call(
        functools.partial(_mix_ffn_kernel, n_in=n_in, layer=layer),
        grid=(m // tm,),
        in_specs=in_specs,
        out_specs=pl.BlockSpec((tm, d), lambda i: (i, 0)),
        out_shape=jax.ShapeDtypeStruct((m, d), F32),
        scratch_shapes=([pltpu.VMEM((tm, dff + d), BF16)] + [pltpu.VMEM((k_piece, d), BF16)] * n_in
                        + _ffn_weight_scratch(d, dff)),
        compiler_params=_params("arbitrary"),
        name="mix_ffn",
    )(*a_list, *([wo_all] * n_in), x, gains, w1_all, w3_all, w2_all)


def _sub_blocks(a):
    return [a[i * SUB:(i + 1) * SUB] for i in range(NSUB)]


def _gla_running_log(lf_hi, lf_lo, tri2):
    return jnp.dot(tri2, jnp.concatenate([lf_hi, lf_lo], axis=0), preferred_element_type=F32)


def _gla_scores(q, k, g, v_t, rev):
    L, K = q.shape
    order = list(range(NSUB))[::-1] if rev else list(range(NSUB))
    last, mid = (0, SUB // 2) if rev else (SUB - 1, SUB // 2 - 1)
    g_p = [_sub_blocks(g)[i] for i in order]
    q_p = [_sub_blocks(q)[i] for i in order]
    k_p = [_sub_blocks(k)[i] for i in order]
    tot = [gp[last:last + 1] for gp in g_p]
    ref = [gp[mid:mid + 1] for gp in g_p]

    qr = [q_p[p] * jnp.exp(g_p[p] - ref[p]) for p in range(NSUB)]
    kr = [k_p[p] * jnp.exp(ref[p] - g_p[p]) for p in range(NSUB)]
    qs = [q_p[p] * jnp.exp(g_p[p]) for p in range(NSUB)]
    ko = [k_p[p] * jnp.exp(tot[p] - g_p[p]) for p in range(NSUB)]
    e1, e2, e3 = jnp.exp(tot[1]), jnp.exp(tot[2]), jnp.exp(tot[3])
    e23 = e2 * e3
    q_in = [qs[0], qs[1] * jnp.exp(tot[0]), qs[2] * jnp.exp(tot[0] + tot[1]),
            qs[3] * jnp.exp(tot[0] + tot[1] + tot[2])]
    k_out = [ko[0] * (e1 * e23), ko[1] * e23, ko[2] * e3, ko[3]]
    d_all = jnp.exp(tot[0] + tot[1] + tot[2] + tot[3])

    zero = jnp.zeros((SUB, K), F32)

    def rows(pieces):
        by_index = [None] * NSUB
        for p, i in enumerate(order):
            by_index[i] = pieces[p]
        return jnp.concatenate(by_index, axis=0).astype(BF16)

    lhs_off = jnp.concatenate([rows([zero, qs[1], zero, zero]),
                               rows([zero, zero, zero, qs[3]]),
                               rows([zero, zero, qs[2], qs[3] * e2])], axis=1)
    rhs_off = jnp.concatenate([rows([ko[0], zero, zero, zero]),
                               rows([zero, zero, ko[2], zero]),
                               rows([ko[0] * e1, ko[1], zero, zero])], axis=1)
    x_diag = lax.dot_general(rows(qr), rows(kr), NT_DIMS, preferred_element_type=F32)
    x_off = lax.dot_general(lhs_off, rhs_off, NT_DIMS, preferred_element_type=F32)
    kv = jnp.dot(v_t, rows(k_out), preferred_element_type=F32)
    return x_diag, x_off, rows(q_in), kv, d_all


def _gla_kernel(q_ref, kf_ref, kb_ref, v_ref, gate_ref, hif_ref, hib_ref, lof_ref, lob_ref, og_ref, o_ref,
                lhs_ref, kv_ref, dall_ref, stb_ref, vt_ref, tri_ref, diag_ref):
    t_len = q_ref.shape[1]
    L = GLA_CHUNK
    nc = t_len // L

    r = lax.broadcasted_iota(jnp.int32, (L, L), 0)
    c = lax.broadcasted_iota(jnp.int32, (L, L), 1)
    same = (r >> SUB_SHIFT) == (c >> SUB_SHIFT)
    for d, keep in enumerate((same & (c <= r), same & (c >= r))):
        ones = jnp.where(keep, 1.0, 0.0)
        diag_ref[d] = ones
        tri_ref[d] = jnp.concatenate([ones, ones], axis=1).astype(BF16)

    def chunk_rows(n):
        return pl.ds(pl.multiple_of(n * L, L), L)

    local_chunks = min(CHUNKS_PER_STEP, nc)
    emit_chunks = min(EMIT_CHUNKS_PER_STEP, nc)
    assert nc % local_chunks == 0 and nc % emit_chunks == 0

    def local_part(i, carry):
        chunks = [i * local_chunks + u for u in range(local_chunks)]
        jobs = [(u, d) for u in range(local_chunks) for d in (0, 1)]
        k_refs, hi_refs, lo_refs = (kf_ref, kb_ref), (hif_ref, hib_ref), (lof_ref, lob_ref)
        gates = []
        for u, d in jobs:
            rs = chunk_rows(chunks[u])
            g = _gla_running_log(hi_refs[d][0, rs, :], lo_refs[d][0, rs, :], tri_ref[d])
            gates.append((k_refs[d][0, rs, :].astype(F32), g))
        v_ts, qs = [], []
        for n in chunks:
            rs = chunk_rows(n)
            v_ts.append(v_ref[0, rs, :].astype(F32).T.astype(BF16))
            vt_ref[n] = v_ts[-1]
            qs.append(q_ref[0, rs, :].astype(F32))
        scores = [_gla_scores(qs[u], k, g, v_ts[u], rev=bool(d)) for (u, d), (k, g) in zip(jobs, gates)]
        for u, n in enumerate(chunks):
            a_sum, q_ins = None, []
            for (uu, d), (x_diag, x_off, q_in, kv, d_all) in zip(jobs, scores):
                if uu != u:
                    continue
                a = jnp.where(diag_ref[d] != 0.0, x_diag, x_off)
                a_sum = a if a_sum is None else a_sum + a
                q_ins.append(q_in)
                kv_ref[d, n] = kv
                dall_ref[d, n] = d_all
            lhs_ref[n] = jnp.concatenate([a_sum.astype(BF16)] + q_ins, axis=1)
        return carry

    lax.fori_loop(0, nc // local_chunks, local_part, 0)

    zero = jnp.zeros((A_VDIM, A_KDIM), F32)

    def backward_state(i, st):
        n = nc - 1 - i
        stb_ref[n] = st.astype(BF16)
        return st * dall_ref[1, n] + kv_ref[1, n]

    lax.fori_loop(0, nc, backward_state, zero)

    ones = jnp.ones((2 * A_VDIM, A_VDIM), BF16)

    def emit(i, st):
        chunks = [i * emit_chunks + u for u in range(emit_chunks)]
        outs = []
        for n in chunks:
            rhs_t = jnp.concatenate([vt_ref[n], st.astype(BF16), stb_ref[n]], axis=1)
            outs.append(lax.dot_general(lhs_ref[n], rhs_t, NT_DIMS, preferred_element_type=F32))
            st = st * dall_ref[0, n] + kv_ref[0, n]
        sqs = []
        for o in outs:
            o2 = o * o
            hi = o2.astype(BF16)
            lo = (o2 - hi.astype(F32)).astype(BF16)
            sqs.append(jnp.dot(jnp.concatenate([hi, lo], axis=1), ones, preferred_element_type=F32))
        for n, o, sq in zip(chunks, outs, sqs):
            rs = chunk_rows(n)
            o = o * lax.rsqrt(sq * (1.0 / A_VDIM) + EPS) * og_ref[0]
            o_ref[0, rs, :] = (o * gate_ref[0, rs, :].astype(F32)).astype(o_ref.dtype)
        return st

    lax.fori_loop(0, nc // emit_chunks, emit, zero)


def _gla(proj, lf, out_gain):
    b, t, _ = proj.shape
    kb = A_KDIM
    L = GLA_CHUNK
    assert A_KDIM == A_VDIM == L and t % (2 * L) == 0
    nc = t // L

    def col(group):
        return pl.BlockSpec((1, t, kb), lambda bi, h, group=group: (bi, 0, group * A_HEADS + h))

    return pl.pallas_call(
        _gla_kernel,
        grid=(b, A_HEADS),
        in_specs=[col(0), col(1), col(2), col(3), col(4), col(0), col(1), col(2), col(3),
                  pl.BlockSpec((1, 1, kb), lambda bi, h: (h, 0, 0))],
        out_specs=pl.BlockSpec((1, t, kb), lambda bi, h: (bi, 0, h)),
        out_shape=jax.ShapeDtypeStruct((b, t, A_WIDTH), BF16),
        scratch_shapes=[pltpu.VMEM((nc, L, 3 * L), BF16),
                        pltpu.VMEM((2, nc, A_VDIM, A_KDIM), F32),
                        pltpu.VMEM((2, nc, 1, A_KDIM), F32),
                        pltpu.VMEM((nc, A_VDIM, A_KDIM), BF16),
                        pltpu.VMEM((nc, A_VDIM, L), BF16),
                        pltpu.VMEM((2, L, 2 * L), BF16), pltpu.VMEM((2, L, L), F32)],
        compiler_params=_params("parallel", "parallel"),
        name="gla",
    )(proj, proj, proj, proj, proj, lf, lf, lf, lf, out_gain.reshape(A_HEADS, 1, A_VDIM))


def _dft_tables(t_len):
    n = B_GDIM
    kk = (np.arange(n)[:, None] * np.arange(n)[None, :]) % n
    ang = 2.0 * np.pi * kk / n
    lane = np.concatenate([np.cos(ang), np.sin(ang)], axis=1).astype(np.float32)
    half = t_len // 2
    t_out = np.arange(half)[:, None]
    nn = np.arange(half)[None, :]
    tabs = []
    for t_in in (2 * nn, 2 * nn + 1):
        a = 2.0 * np.pi * ((t_in * t_out) % t_len) / t_len
        tabs.append(np.concatenate([np.cos(a), -np.sin(a)], axis=1).astype(np.float32))
    return lane, tabs[0], tabs[1]


def _fnet_kernel(u_ref, lane_ref, even_ref, odd_ref, o_ref, y_ref):
    t_len = u_ref.shape[1]
    half = t_len // 2
    lane = lane_ref[...]
    for g in range(B_GROUPS):
        ug = u_ref[0, :, g * B_GDIM:(g + 1) * B_GDIM].astype(BF16)
        y = jnp.dot(ug, lane, preferred_element_type=F32)
        y_ref[2 * g] = y[:, :B_GDIM]
        y_ref[2 * g + 1] = y[:, B_GDIM:]

    def rows_of_parity(parity):
        sel = pl.ds(parity, half, stride=2)
        parts = [jnp.concatenate([y_ref[2 * g + cs, sel, :] for g in range(B_GROUPS)], axis=1) for cs in (0, 1)]
        return jnp.concatenate(parts, axis=0).astype(BF16)

    norm = 1.0 / np.sqrt(float(t_len) * B_GDIM)
    e = jnp.dot(even_ref[...], rows_of_parity(0), preferred_element_type=F32)
    o = jnp.dot(odd_ref[...], rows_of_parity(1), preferred_element_type=F32)
    o_ref[0, :half, :] = ((e + o) * norm).astype(o_ref.dtype)
    o_ref[0, half:, :] = ((e - o) * norm).astype(o_ref.dtype)


def _fnet(proj):
    b, t, n_all = proj.shape
    half = t // 2
    lane, even, odd = (jnp.asarray(a).astype(BF16) for a in _dft_tables(t))
    return pl.pallas_call(
        _fnet_kernel,
        grid=(b,),
        in_specs=[
            pl.BlockSpec((1, t, B_WIDTH), lambda bi: (bi, 0, n_all // B_WIDTH - 1)),
            _resident((B_GDIM, 2 * B_GDIM), lambda bi: (0, 0)),
            _resident((half, t), lambda bi: (0, 0)),
            _resident((half, t), lambda bi: (0, 0)),
        ],
        out_specs=pl.BlockSpec((1, t, B_WIDTH), lambda bi: (bi, 0, 0)),
        out_shape=jax.ShapeDtypeStruct((b, t, B_WIDTH), BF16),
        scratch_shapes=[pltpu.VMEM((2 * B_GROUPS, t, B_GDIM), F32)],
        compiler_params=_params("arbitrary"),
        name="fnet",
    )(proj, lane, even, odd)


ATTN_ITEMS_AFTER_LAST_OUTPUT = 3
ATTN_SPAN = QBLOCK + 2 * WINDOW
ATTN_GROUP_HEADS = [[kv * C_GROUP + g for g in range(C_GROUP)] for kv in range(C_KV)]


def _attn_write_bias_tables(bias_ref):
    c = lax.broadcasted_iota(jnp.int32, (ATTN_SPAN, QBLOCK), 0)
    r = lax.broadcasted_iota(jnp.int32, (ATTN_SPAN, QBLOCK), 1)
    dist = jnp.abs(r + WINDOW - c)
    dist_f = dist.astype(F32)
    for variant, (c_lo, c_hi) in enumerate(((WINDOW, ATTN_SPAN), (0, ATTN_SPAN), (0, WINDOW + QBLOCK))):
        valid = (dist <= WINDOW) & (c >= c_lo) & (c < c_hi)
        bias_ref[variant] = jnp.where(valid, -LOG2_E * dist_f, NEG_BIG)


def _attn_block_stages(q_blk, k_span, v_span, variant, sink_ref, bias_ref, store):
    scale = C_HDIM ** -0.5 * LOG2_E
    tn_dims = (((0,), (0,)), ((), ()))
    held = {}

    def logits():
        held["s"] = []
        for kv in range(C_KV):
            q = jnp.concatenate([q_blk(h) for h in ATTN_GROUP_HEADS[kv]], axis=0)
            q = (q.astype(F32) * scale).astype(BF16)
            held["s"].append(lax.dot_general(k_span(kv), q, NT_DIMS, preferred_element_type=F32))

    def softmax():
        held["p"], held["denom"] = [], []
        for kv in range(C_KV):
            base = bias_ref[variant]
            slopes = [2.0 ** (-8.0 * (h + 1) / C_HEADS) for h in ATTN_GROUP_HEADS[kv]]
            s = held["s"][kv] + jnp.concatenate([slope * base for slope in slopes], axis=1)
            sink = jnp.concatenate([jnp.full((1, QBLOCK), sink_ref[h] * LOG2_E, F32)
                                    for h in ATTN_GROUP_HEADS[kv]], axis=1)
            m = jnp.maximum(jnp.max(s, axis=0, keepdims=True), sink)
            p = jnp.exp2(s - m)
            held["denom"].append(jnp.sum(p, axis=0, keepdims=True) + jnp.exp2(sink - m))
            held["p"].append(p.astype(BF16))

    def output():
        outs_t = [lax.dot_general(v_span(kv), held["p"][kv], tn_dims, preferred_element_type=F32)
                  for kv in range(C_KV)]
        outs = []
        for kv in range(C_KV):
            o_t = outs_t[kv] / held["denom"][kv]
            outs += [o_t[:, g * QBLOCK:(g + 1) * QBLOCK].T for g in range(C_GROUP)]
        store(jnp.concatenate(outs, axis=1))

    return logits, softmax, output


def _attn_ffn_kernel(sink_ref, q_ref, kp_ref, km_ref, kn_ref, vp_ref, vm_ref, vn_ref,
                     wo_ref, x_ref, gains_ref, w1_hbm, w3_hbm, w2_hbm, o_ref,
                     bias_ref, att_ref, act_ref, wob_ref, w1b, w3b, w2b, stage_up, stage_down, sem_up, sem_down,
                     *, blocks_per_seq, layer):
    s = pl.program_id(0)
    tm = q_ref.shape[0]
    nq = tm // QBLOCK
    slot_attn = s % 2
    slot_ffn = (s + 1) % 2

    @pl.when(s == 0)
    def _():
        _attn_write_bias_tables(bias_ref)

    _cast_weight_once(wo_ref, wob_ref)
    _load_ffn_weights_once(layer, (w1_hbm, w3_hbm, w2_hbm), (w1b, w3b, w2b), stage_up, stage_down, sem_up, sem_down)

    seq_pos = jnp.minimum(s, pl.num_programs(0) - 2) % blocks_per_seq
    first = jnp.where(seq_pos == 0, 0, 1)
    last = jnp.where(seq_pos == blocks_per_seq - 1, 2, 1)

    def key_piece(prev_ref, main_ref, next_ref, i, cols):
        if i == 0:
            return prev_ref[:, cols]
        if i == nq + 1:
            return next_ref[:, cols]
        return main_ref[(i - 1) * QBLOCK:i * QBLOCK, cols]

    def span_of(prev_ref, main_ref, next_ref, jq):
        def get(kv):
            cols = slice(kv * C_HDIM, (kv + 1) * C_HDIM)
            return jnp.concatenate([key_piece(prev_ref, main_ref, next_ref, jq + i, cols) for i in range(3)], axis=0)
        return get

    def attention_stages():
        stages = []
        for jq in range(nq):
            rows = slice(jq * QBLOCK, (jq + 1) * QBLOCK)
            variant = first if jq == 0 else (last if jq == nq - 1 else 1)

            def store(o, rows=rows):
                att_ref[slot_attn, rows, :] = o.astype(att_ref.dtype)

            stages.append(_attn_block_stages(
                lambda h, rows=rows: q_ref[rows, h * C_HDIM:(h + 1) * C_HDIM],
                span_of(kp_ref, km_ref, kn_ref, jq), span_of(vp_ref, vm_ref, vn_ref, jq),
                variant, sink_ref, bias_ref, store))
        return stages

    def mixer_items():
        return _mix_ffn_items(lambda p, rs: att_ref[slot_ffn, rs, :], [wob_ref],
                              x_ref, gains_ref, w1b, w3b, w2b, o_ref, act_ref)

    last_step = pl.num_programs(0) - 1

    @pl.when(s == 0)
    def _():
        for logits, softmax, output in attention_stages():
            logits()
            softmax()
            output()

    @pl.when(s == last_step)
    def _():
        for item in mixer_items():
            item()

    @pl.when((s > 0) & (s < last_step))
    def _():
        stages, items = attention_stages(), mixer_items()
        items.pop(0)()
        share = -(-len(items) // nq)
        pending_output = None
        for jq, (logits, softmax, output) in enumerate(stages):
            logits()
            if pending_output is not None:
                pending_output()
            softmax()
            mine = items[jq * share:(jq + 1) * share]
            after = ATTN_ITEMS_AFTER_LAST_OUTPUT if jq == nq - 1 else 0
            for item in mine[:len(mine) - after]:
                item()
            if jq == nq - 1:
                output()
            else:
                pending_output = output
            for item in mine[len(mine) - after:]:
                item()


def _attn_ffn(qkv, sink, t, wo_all, wo_layer, x, gains, w1_all, w3_all, w2_all, layer):
    m, d = x.shape
    dff = w1_all.shape[2]
    tm = _row_tile(m)
    assert WINDOW == QBLOCK and tm % QBLOCK == 0 and t % tm == 0 and t // QBLOCK >= 2 and dff % COL_TILE == 0
    qw = C_HEADS * C_HDIM
    kw = C_KV * C_HDIM
    assert qw % kw == 0 and wo_all.shape[1] == qw
    k_col, v_col = qw // kw, qw // kw + 1
    n = m // tm
    per = tm // QBLOCK
    n_qb = m // QBLOCK

    def attn_block(s):
        return jnp.minimum(s, n - 1)

    def main_spec(width, col):
        return pl.BlockSpec((tm, width), lambda s: (attn_block(s), col))

    def edge_spec(col, after):
        def index(s):
            qb = attn_block(s) * per + (per if after else -1)
            return (jnp.clip(qb, 0, n_qb - 1), col)
        return pl.BlockSpec((QBLOCK, kw), index)

    def ffn_rows(s):
        return (jnp.maximum(s - 1, 0), 0)

    return pl.pallas_call(
        functools.partial(_attn_ffn_kernel, blocks_per_seq=t // tm, layer=layer),
        grid=(n + 1,),
        in_specs=[pl.BlockSpec(memory_space=pltpu.SMEM),
                  main_spec(qw, 0),
                  edge_spec(k_col, False), main_spec(kw, k_col), edge_spec(k_col, True),
                  edge_spec(v_col, False), main_spec(kw, v_col), edge_spec(v_col, True),
                  _resident((None, qw, d), lambda s: (wo_layer, 0, 0)),
                  pl.BlockSpec((tm, d), ffn_rows),
                  pl.BlockSpec(gains.shape, lambda s: (0, 0)),
                  pl.BlockSpec(memory_space=pl.ANY), pl.BlockSpec(memory_space=pl.ANY),
                  pl.BlockSpec(memory_space=pl.ANY)],
        out_specs=pl.BlockSpec((tm, d), ffn_rows),
        out_shape=jax.ShapeDtypeStruct((m, d), F32),
        scratch_shapes=[pltpu.VMEM((3, ATTN_SPAN, QBLOCK), F32),
                        pltpu.VMEM((2, tm, qw), BF16),
                        pltpu.VMEM((tm, dff + d), BF16),
                        pltpu.VMEM((qw, d), BF16)] + _ffn_weight_scratch(d, dff),
        compiler_params=_params("arbitrary"),
        name="attn_ffn",
    )(sink.astype(F32), qkv, qkv, qkv, qkv, qkv, qkv, qkv, wo_all, x, gains, w1_all, w3_all, w2_all)


def kernel(x, norm_gains, ab_w_in, ab_lb_table, ab_out_gain, ab_w_out, c_w_qkv, c_sink, c_w_out,
           ffn_w1, ffn_w3, ffn_w2):
    b, t, d = x.shape
    m = b * t
    depth = norm_gains.shape[0]
    xf = x.reshape(m, d)
    for layer in range(depth):
        gains = norm_gains[layer]
        if layer % 2 == 0:
            e = layer // 2
            proj, lf = _in_proj(xf, gains[0], ab_w_in, e, ab_lb_table)
            proj = proj.reshape(b, t, AB_IN)
            o = _gla(proj, lf.reshape(b, t, 4 * A_FDIM), ab_out_gain[e]).reshape(m, A_WIDTH)
            fo = _fnet(proj).reshape(m, B_WIDTH)
            xf = _mix_ffn([o, fo], ab_w_out, e, xf, gains, ffn_w1, ffn_w3, ffn_w2, layer)
        else:
            w_idx = layer // 2
            qkv = _norm_matmul(xf, gains[0], c_w_qkv, w_idx, BF16)
            xf = _attn_ffn(qkv, c_sink[w_idx], t, c_w_out, w_idx, xf, gains, ffn_w1, ffn_w3, ffn_w2, layer)
    return xf.reshape(b, t, d)
```
